```python
import math
import jax, jax.numpy as jnp
from jax import lax
import numpy as np

D_MODEL = 1024
BATCH = 2
SEQ = 8192
DEPTH = 4
DEC_BATCH = 128
DEC_SEQ = 1
PAST_LEN = 2048
PAGE_SIZE = 128

N_A_LAYERS = DEPTH // 2
N_B_LAYERS = DEPTH - N_A_LAYERS
POOL_WINDOWS = (2, 4, 8, 16)
N_POOL_GROUPS = len(POOL_WINDOWS)
POOL_WIDTH = D_MODEL
POOL_GROUP_WIDTH = POOL_WIDTH // N_POOL_GROUPS
POOL_STATE = max(POOL_WINDOWS) - 1
ATTN_PATTERNS = ((128, 1), (512, 4), (2048, 16))
N_GROUPS = len(ATTN_PATTERNS)
HEAD_DIM = 64
HEADS_PER_GROUP = D_MODEL // HEAD_DIM
ATTN_WIDTH = HEADS_PER_GROUP * HEAD_DIM
ROT_DIM = HEAD_DIM // 4
ROPE_THETA = 500000.0
SUB_BLOCK = 128
RMS_EPS = 1e-6

kernel_name = 'yoco_pool_dilated_attn_step'


def rmsnorm(x, g):
    xf = x.astype(jnp.float32)
    y = xf * lax.rsqrt(jnp.mean(xf * xf, axis=-1, keepdims=True) + RMS_EPS)
    return (y * g.astype(jnp.float32)).astype(x.dtype)


def rope_partial(x, pos):
    half = ROT_DIM // 2
    inv = 1.0 / (ROPE_THETA ** (jnp.arange(0, ROT_DIM, 2, dtype=jnp.float32) / ROT_DIM))
    ang = pos.astype(jnp.float32)[:, None] * inv[None, :]
    cos = jnp.cos(ang)[None, :, None, :]
    sin = jnp.sin(ang)[None, :, None, :]
    xr = x[..., :ROT_DIM].astype(jnp.float32)
    x1, x2 = xr[..., :half], xr[..., half:]
    rot = jnp.concatenate([x1 * cos - x2 * sin, x1 * sin + x2 * cos], axis=-1).astype(x.dtype)
    return jnp.concatenate([rot, x[..., ROT_DIM:]], axis=-1)


def pool_residual(u, pos):
    uf = u.astype(jnp.float32)
    c = jnp.cumsum(uf, axis=1)
    L = u.shape[1]
    outs = []
    for gi, w in enumerate(POOL_WINDOWS):
        sl = slice(gi * POOL_GROUP_WIDTH, (gi + 1) * POOL_GROUP_WIDTH)
        cg = c[..., sl]
        prev = jnp.pad(cg, ((0, 0), (w, 0), (0, 0)))[:, :L]
        cnt = jnp.minimum(pos + 1, w).astype(jnp.float32)[None, :, None]
        outs.append((cg - prev) / cnt - uf[..., sl])
    return jnp.concatenate(outs, axis=-1)


def pool_in(h, g, w_in):
    proj = rmsnorm(h, g) @ w_in
    return proj[..., :POOL_WIDTH], proj[..., POOL_WIDTH:]


def pool_out(r, gate, w_grp, scale, w_out, dt):
    B, L, _ = r.shape
    rg = r.astype(dt).reshape(B, L, N_POOL_GROUPS, POOL_GROUP_WIDTH)
    z = jnp.einsum('blgc,gcd->blgd', rg, w_grp, preferred_element_type=jnp.float32).reshape(B, L, POOL_WIDTH)
    y = z * scale.astype(jnp.float32) * jax.nn.silu(gate.astype(jnp.float32))
    return y.astype(dt) @ w_out


def shared_kv(h, g, w_kv, pos):
    B, T, _ = h.shape
    kv = (rmsnorm(h, g) @ w_kv).reshape(B, T, N_GROUPS, 2, HEADS_PER_GROUP, HEAD_DIM)
    k = rope_partial(kv[:, :, :, 0].reshape(B, T, N_GROUPS * HEADS_PER_GROUP, HEAD_DIM), pos)
    k = k.reshape(B, T, N_GROUPS, HEADS_PER_GROUP, HEAD_DIM)
    return jnp.stack([k, kv[:, :, :, 1]], axis=3)


def attn_in(h, g, w_in, pos):
    B, T, _ = h.shape
    proj = rmsnorm(h, g) @ w_in
    q = proj[..., :N_GROUPS * ATTN_WIDTH].reshape(B, T, N_GROUPS * HEADS_PER_GROUP, HEAD_DIM)
    q = rope_partial(q, pos).reshape(B, T, N_GROUPS, HEADS_PER_GROUP, HEAD_DIM)
    return q, proj[..., N_GROUPS * ATTN_WIDTH:]


def dilated_attn_prompt(q, k, v, window, dil):
    B, S, H, Dh = q.shape
    n = window // dil
    Ls = S // dil
    nblk = -(-Ls // SUB_BLOCK)
    Lp = nblk * SUB_BLOCK
    scale = 1.0 / math.sqrt(Dh)

    def to_sub(x):
        x = x.reshape(B, Ls, dil, H, Dh).transpose(0, 2, 1, 3, 4)
        x = jnp.pad(x, ((0, 0), (0, 0), (0, Lp - Ls), (0, 0), (0, 0)))
        return x.reshape(B, dil, nblk, SUB_BLOCK, H, Dh)

    def with_prev(x):
        prev = jnp.pad(x, ((0, 0), (0, 0), (1, 0), (0, 0), (0, 0), (0, 0)))[:, :, :-1]
        return jnp.concatenate([prev, x], axis=3)

    qs = to_sub(q)
    kb = with_prev(to_sub(k))
    vb = with_prev(to_sub(v))
    s = jnp.einsum('brnqhd,brnkhd->brnhqk', qs, kb, preferred_element_type=jnp.float32) * scale
    a = jnp.arange(SUB_BLOCK)[:, None]
    c = jnp.arange(2 * SUB_BLOCK)[None, :]
    delta = SUB_BLOCK + a - c
    key_sub = (jnp.arange(nblk)[:, None, None] - 1) * SUB_BLOCK + c[None]
    valid = (delta >= 0)[None] & (delta <= n)[None] & (key_sub >= 0)
    s = jnp.where(valid[None, None, :, None], s, -jnp.inf)
    m = jnp.max(s, axis=-1)
    p = jnp.exp(s - m[..., None])
    l = jnp.sum(p, axis=-1)
    o = jnp.einsum('brnhqk,brnkhd->brnqhd', p, vb.astype(jnp.float32)) / jnp.swapaxes(l, 3, 4)[..., None]
    lse = jnp.swapaxes(m + jnp.log(l), 3, 4)
    o = o.reshape(B, dil, Lp, H, Dh)[:, :, :Ls].transpose(0, 2, 1, 3, 4).reshape(B, S, H, Dh)
    lse = lse.reshape(B, dil, Lp, H)[:, :, :Ls].transpose(0, 2, 1, 3).reshape(B, S, H)
    return o, lse


def gather_rows(cache, new, idx):
    Lc = cache.shape[1]
    T = new.shape[1]
    from_cache = jnp.take(cache, jnp.clip(idx, 0, Lc - 1), axis=1)
    from_new = jnp.take(new, jnp.clip(idx - Lc, 0, T - 1), axis=1)
    return jnp.where((idx < Lc)[None, :, :, None, None, None], from_cache, from_new)


def dilated_attn_sample(q, cache, kv_new, window, dil):
    Lc = cache.shape[1]
    T = q.shape[1]
    n = window // dil
    idx = (Lc + jnp.arange(T))[:, None] - dil * jnp.arange(n + 1)[None, :]
    valid = idx >= 0
    rows = gather_rows(cache, kv_new, jnp.maximum(idx, 0))
    k, v = rows[:, :, :, 0], rows[:, :, :, 1]
    s = jnp.einsum('bthd,btjhd->bthj', q, k, preferred_element_type=jnp.float32) / math.sqrt(q.shape[-1])
    s = jnp.where(valid[None, :, None, :], s, -jnp.inf)
    m = jnp.max(s, axis=-1)
    p = jnp.exp(s - m[..., None])
    l = jnp.sum(p, axis=-1)
    o = jnp.einsum('bthj,btjhd->bthd', p, v.astype(jnp.float32)) / l[..., None]
    return o, m + jnp.log(l)


def attn_out(outs, gate, w_out, dt):
    o = jnp.stack([oo for oo, _ in outs], axis=0)
    lse = jnp.stack([ll for _, ll in outs], axis=0)
    wts = jax.nn.softmax(lse, axis=0)
    comb = jnp.sum(wts[..., None] * o, axis=0)
    B, T = comb.shape[:2]
    y = comb.reshape(B, T, ATTN_WIDTH) * jax.nn.silu(gate.astype(jnp.float32))
    return y.astype(dt) @ w_out


def setup_inputs(seed: int = 0) -> dict:
    key = jax.random.key(seed)
    ks = jax.random.split(key, 20)

    def nrm(k, shape, scale):
        return jax.random.normal(k, shape, jnp.float32) * scale

    cache_shapes = [(DEC_BATCH, min(w, PAST_LEN), 2, HEADS_PER_GROUP, HEAD_DIM) for w, _ in ATTN_PATTERNS]
    return {
        'x_prompt': nrm(ks[0], (BATCH, SEQ, D_MODEL), 1.0),
        'x_sample': nrm(ks[1], (DEC_BATCH, DEC_SEQ, D_MODEL), 1.0),
        'state_pool': nrm(ks[2], (N_A_LAYERS, DEC_BATCH, POOL_STATE, POOL_WIDTH), 1.0),
        'cache_kv_w128': nrm(ks[3], cache_shapes[0], 1.0),
        'cache_kv_w512': nrm(ks[4], cache_shapes[1], 1.0),
        'cache_kv_w2048': nrm(ks[5], cache_shapes[2], 1.0),
        'g_a': 1.0 + nrm(ks[6], (N_A_LAYERS, D_MODEL), 0.05),
        'w_a_in': nrm(ks[7], (N_A_LAYERS, D_MODEL, 2 * POOL_WIDTH), D_MODEL ** -0.5),
        'w_a_group': nrm(ks[8], (N_A_LAYERS, N_POOL_GROUPS, POOL_GROUP_WIDTH, POOL_GROUP_WIDTH), POOL_GROUP_WIDTH ** -0.5),
        'a_scale': 1.0 + nrm(ks[9], (N_A_LAYERS, POOL_WIDTH), 0.1),
        'w_a_out': nrm(ks[10], (N_A_LAYERS, POOL_WIDTH, D_MODEL), POOL_WIDTH ** -0.5),
        'g_kv': 1.0 + nrm(ks[11], (D_MODEL,), 0.05),
        'w_kv': nrm(ks[12], (D_MODEL, N_GROUPS * 2 * ATTN_WIDTH), D_MODEL ** -0.5),
        'g_b': 1.0 + nrm(ks[13], (N_B_LAYERS, D_MODEL), 0.05),
        'w_b_in': nrm(ks[14], (N_B_LAYERS, D_MODEL, N_GROUPS * ATTN_WIDTH + ATTN_WIDTH), D_MODEL ** -0.5),
        'w_b_out': nrm(ks[15], (N_B_LAYERS, ATTN_WIDTH, D_MODEL), ATTN_WIDTH ** -0.5),
        'g_final': 1.0 + nrm(ks[16], (D_MODEL,), 0.05),
    }


def reference(x_prompt, x_sample, state_pool, cache_kv_w128, cache_kv_w512, cache_kv_w2048,
              g_a, w_a_in, w_a_group, a_scale, w_a_out, g_kv, w_kv, g_b, w_b_in, w_b_out, g_final):
    dt = x_prompt.dtype
    S = x_prompt.shape[1]
    T = x_sample.shape[1]
    pos_p = jnp.arange(S, dtype=jnp.int32)
    pos_s = PAST_LEN + jnp.arange(T, dtype=jnp.int32)
    pos_buf = PAST_LEN - POOL_STATE + jnp.arange(POOL_STATE + T, dtype=jnp.int32)
    caches = (cache_kv_w128, cache_kv_w512, cache_kv_w2048)
    keep = [min(w, S) for w, _ in ATTN_PATTERNS]

    h_p, h_s = x_prompt, x_sample
    pool_new_p, pool_new_s = [], []
    kv_p, kv_s = None, None
    for layer in range(DEPTH):
        if layer < N_A_LAYERS:
            i = layer
            u_p, gate_p = pool_in(h_p, g_a[i], w_a_in[i])
            u_s, gate_s = pool_in(h_s, g_a[i], w_a_in[i])
            u_full = jnp.concatenate([state_pool[i].astype(u_s.dtype), u_s], axis=1)
            r_p = pool_residual(u_p, pos_p)
            r_s = pool_residual(u_full, pos_buf)[:, POOL_STATE:]
            h_p = h_p + pool_out(r_p, gate_p, w_a_group[i], a_scale[i], w_a_out[i], dt)
            h_s = h_s + pool_out(r_s, gate_s, w_a_group[i], a_scale[i], w_a_out[i], dt)
            pool_new_p.append(u_p[:, -POOL_STATE:])
            pool_new_s.append(u_full[:, -POOL_STATE:])
        else:
            if layer == N_A_LAYERS:
                kv_p = shared_kv(h_p, g_kv, w_kv, pos_p)
                kv_s = shared_kv(h_s, g_kv, w_kv, pos_s)
            j = layer - N_A_LAYERS
            q_p, gate_p = attn_in(h_p, g_b[j], w_b_in[j], pos_p)
            q_s, gate_s = attn_in(h_s, g_b[j], w_b_in[j], pos_s)
            outs_p = [dilated_attn_prompt(q_p[:, :, g], kv_p[:, :, g, 0], kv_p[:, :, g, 1], w, d)
                      for g, (w, d) in enumerate(ATTN_PATTERNS)]
            outs_s = [dilated_attn_sample(q_s[:, :, g], caches[g], kv_s[:, :, g], w, d)
                      for g, (w, d) in enumerate(ATTN_PATTERNS)]
            h_p = h_p + attn_out(outs_p, gate_p, w_b_out[j], dt)
            h_s = h_s + attn_out(outs_s, gate_s, w_b_out[j], dt)

    y_prompt = rmsnorm(h_p, g_final)
    y_sample = rmsnorm(h_s, g_final)
    pool_prompt = jnp.stack(pool_new_p, axis=0)
    pool_sample = jnp.stack(pool_new_s, axis=0)
    return (y_prompt, y_sample, pool_prompt, pool_sample,
            kv_p[:, S - keep[0]:, 0], kv_s[:, :, 0],
            kv_p[:, S - keep[1]:, 1], kv_s[:, :, 1],
            kv_p[:, S - keep[2]:, 2], kv_s[:, :, 2])
```

```python
import functools
import math

import jax
import jax.numpy as jnp
from jax import lax
from jax.experimental import pallas as pl
from jax.experimental.pallas import tpu as pltpu

D_MODEL = 1024
HEAD_DIM = 64
N_HEADS = D_MODEL // HEAD_DIM
ROT_DIM = HEAD_DIM // 4
ROPE_THETA = 500000.0
POOL_WINDOWS = (2, 4, 8, 16)
POOL_GROUP_WIDTH = D_MODEL // len(POOL_WINDOWS)
POOL_STATE = max(POOL_WINDOWS) - 1
ATTN_PATTERNS = ((128, 1), (512, 4), (2048, 16))
N_GROUPS = len(ATTN_PATTERNS)
SUB_BLOCK = 128
RMS_EPS = 1e-6
PAST_LEN = 2048

LANES = 128
ATTN_TILE = SUB_BLOCK * max(d for _, d in ATTN_PATTERNS)
POOL_HIST = 32
VMEM_LIMIT = 56 * 1024 * 1024

F32 = jnp.float32
BF16 = jnp.bfloat16


def _rmsnorm(x, g):
    ms = jnp.mean(x * x, axis=-1, keepdims=True)
    return x * lax.rsqrt(ms + RMS_EPS) * g


def _silu(x):
    return x * (1.0 / (1.0 + jnp.exp(-x)))


def _const_spec(shape):
    return pl.BlockSpec(shape, lambda *_: (0,) * len(shape), pipeline_mode=pl.Buffered(1))


def _params(sem):
    return pltpu.CompilerParams(dimension_semantics=sem, vmem_limit_bytes=VMEM_LIMIT)


def _pool_kernel(h_ref, g_ref, win_ref, wgrp_ref, scale_ref, wout_ref, out_ref, ulast_ref,
                 ext_ref, lvl_ref, *, tm):
    i = pl.program_id(1)
    n_i = pl.num_programs(1)
    hist = POOL_HIST

    @pl.when(i == 0)
    def _():
        ext_ref[0:hist, :] = jnp.zeros((hist, D_MODEL), F32)

    x = h_ref[0]
    hn = _rmsnorm(x, g_ref[...]).astype(BF16)
    proj = jnp.dot(hn, win_ref[...], preferred_element_type=F32)
    gate = proj[:, D_MODEL:]
    ext_ref[hist:hist + tm, :] = proj[:, :D_MODEL]

    total = hist + tm
    pos = i * tm + lax.broadcasted_iota(jnp.int32, (tm, 1), 0)
    src = ext_ref
    z_parts = []
    for k, w in enumerate(POOL_WINDOWS):
        shift = w // 2
        lo = 8 * (k + 1)
        c0 = k * POOL_GROUP_WIDTH
        cur = src[lo:total, c0:] + src[lo - shift:total - shift, c0:]
        dst = lvl_ref.at[k % 2]
        dst[lo:total, c0:] = cur
        inv_cnt = 1.0 / jnp.minimum(pos + 1, w).astype(F32)
        wsum = dst[hist:total, c0:c0 + POOL_GROUP_WIDTH]
        r = wsum * inv_cnt - ext_ref[hist:total, c0:c0 + POOL_GROUP_WIDTH]
        z_parts.append(jnp.dot(r.astype(BF16), wgrp_ref[k], preferred_element_type=F32))
        src = dst
    z = jnp.concatenate(z_parts, axis=-1)
    y = z * scale_ref[...] * _silu(gate)
    out_ref[0] = x + jnp.dot(y.astype(BF16), wout_ref[...], preferred_element_type=F32)

    @pl.when(i == n_i - 1)
    def _():
        ulast_ref[0] = ext_ref[hist + tm - 16:hist + tm, :]

    ext_ref[0:hist, :] = ext_ref[tm:tm + hist, :]


def _pool_layer(h, g, w_in, w_grp, scale, w_out, *, tm):
    b, s, d = h.shape
    kern = functools.partial(_pool_kernel, tm=tm)
    return pl.pallas_call(
        kern,
        grid=(b, s // tm),
        in_specs=[
            pl.BlockSpec((1, tm, d), lambda bi, i: (bi, i, 0)),
            _const_spec((1, d)),
            _const_spec((d, 2 * d)),
            _const_spec((len(POOL_WINDOWS), POOL_GROUP_WIDTH, POOL_GROUP_WIDTH)),
            _const_spec((1, d)),
            _const_spec((d, d)),
        ],
        out_specs=[
            pl.BlockSpec((1, tm, d), lambda bi, i: (bi, i, 0)),
            pl.BlockSpec((1, 16, d), lambda bi, i: (bi, 0, 0)),
        ],
        out_shape=[jax.ShapeDtypeStruct((b, s, d), F32), jax.ShapeDtypeStruct((b, 16, d), F32)],
        scratch_shapes=[pltpu.VMEM((POOL_HIST + tm, d), F32), pltpu.VMEM((2, POOL_HIST + tm, d), F32)],
        compiler_params=_params(("arbitrary", "arbitrary")),
        name="pool_layer",
    )(h, g, w_in, w_grp, scale, w_out)


def _spool_kernel(h_ref, st_ref, g_ref, win_ref, wgrp_ref, scale_ref, wout_ref, out_ref, nst_ref):
    x = h_ref[...]
    hn = _rmsnorm(x, g_ref[...]).astype(BF16)
    proj = jnp.dot(hn, win_ref[...], preferred_element_type=F32)
    u = proj[:, :D_MODEL]
    gate = proj[:, D_MODEL:]
    z_parts = []
    for k, w in enumerate(POOL_WINDOWS):
        c0 = k * POOL_GROUP_WIDTH
        uk = u[:, c0:c0 + POOL_GROUP_WIDTH]
        tot = uk
        for j in range(1, w):
            off = (POOL_STATE - j) * D_MODEL + c0
            tot = tot + st_ref[:, off:off + POOL_GROUP_WIDTH]
        cnt = float(min(PAST_LEN + 1, w))
        r = tot / cnt - uk
        z_parts.append(jnp.dot(r.astype(BF16), wgrp_ref[k], preferred_element_type=F32))
    z = jnp.concatenate(z_parts, axis=-1)
    y = z * scale_ref[...] * _silu(gate)
    out_ref[...] = x + jnp.dot(y.astype(BF16), wout_ref[...], preferred_element_type=F32)
    keep = (POOL_STATE - 1) * D_MODEL
    nst_ref[:, 0:keep] = st_ref[:, D_MODEL:]
    nst_ref[:, keep:] = u


def _spool_layer(h, state2d, g, w_in, w_grp, scale, w_out):
    n, d = h.shape
    return pl.pallas_call(
        _spool_kernel,
        out_shape=[jax.ShapeDtypeStruct((n, d), F32), jax.ShapeDtypeStruct(state2d.shape, F32)],
        compiler_params=pltpu.CompilerParams(vmem_limit_bytes=VMEM_LIMIT),
        name="pool_layer_sample",
    )(h, state2d, g, w_in, w_grp, scale, w_out)


def _rope_tables(pos):
    half = ROT_DIM // 2
    inv = 1.0 / (ROPE_THETA ** (jnp.arange(0, ROT_DIM, 2, dtype=jnp.float32) / ROT_DIM))
    ang = pos.astype(jnp.float32)[:, None] * inv[None, :]
    cos, sin = jnp.cos(ang), jnp.sin(ang)
    t = pos.shape[0]
    rest = HEAD_DIM - ROT_DIM
    cos_h = jnp.concatenate([cos, cos, jnp.ones((t, rest), F32)], axis=1)
    sin_a = jnp.concatenate([jnp.zeros((t, half), F32), sin, jnp.zeros((t, rest), F32)], axis=1)
    sin_b = jnp.concatenate([-sin, jnp.zeros((t, half + rest), F32)], axis=1)
    reps = LANES // HEAD_DIM
    return tuple(jnp.tile(a, (1, reps)) for a in (cos_h, sin_a, sin_b))


def _proj_kernel(h_ref, g_ref, w_ref, cos_ref, sa_ref, sb_ref, *rest, tm, chunks, f32_outs, n_bf16):
    bf_refs = rest[:n_bf16]
    f32_refs = rest[n_bf16:n_bf16 + len(f32_outs)]
    y_ref = rest[-1]
    i = pl.program_id(1)
    n_i = pl.num_programs(1)
    half = ROT_DIM // 2

    hn = _rmsnorm(h_ref[0], g_ref[...]).astype(BF16)
    bi = 0
    n_lt = D_MODEL // LANES
    for c, (rope, scale, dil, bf_out, f32_slot) in enumerate(chunks):
        y = jnp.dot(hn, w_ref[:, c * D_MODEL:(c + 1) * D_MODEL], preferred_element_type=F32)
        for lt in range(n_lt):
            ys = y[:, lt * LANES:(lt + 1) * LANES]
            if rope:
                ys = (ys * cos_ref[...] + pltpu.roll(ys, half, 1) * sa_ref[...]
                      + pltpu.roll(ys, LANES - half, 1) * sb_ref[...])
            y_ref[lt] = ys * scale if scale != 1.0 else ys
        if f32_slot is not None:
            oi, off = f32_slot
            keep = f32_outs[oi][0]
            kb = min(tm, keep)
            first = n_i - keep // kb

            @pl.when(i >= first)
            def _(oi=oi, off=off, kb=kb):
                for lt in range(n_lt):
                    f32_refs[oi][0, :, off + lt * LANES:off + (lt + 1) * LANES] = y_ref[lt, tm - kb:tm, :]
        if bf_out:
            o_ref = bf_refs[bi]
            bi += 1
            for lt in range(n_lt):
                cols = slice(lt * LANES, (lt + 1) * LANES)
                if dil == 1:
                    o_ref[0, 0, :, cols] = y_ref[lt].astype(BF16)
                else:
                    for r in range(dil):
                        o_ref[0, r, :, cols] = y_ref[lt, pl.ds(r, tm // dil, stride=dil), :].astype(BF16)


def _proj(h, g, w, tables, chunks, f32_outs, *, tm):
    b, s, d = h.shape
    n_i = s // tm
    out_shapes, out_specs = [], []
    for rope, scale, dil, bf_out, f32_slot in chunks:
        if bf_out:
            out_shapes.append(jax.ShapeDtypeStruct((b, dil, s // dil, d), BF16))
            out_specs.append(pl.BlockSpec((1, dil, tm // dil, d), lambda bi, i: (bi, 0, i, 0)))
    n_bf16 = len(out_shapes)
    for keep, width in f32_outs:
        kb = min(tm, keep)
        first = n_i - keep // kb
        out_shapes.append(jax.ShapeDtypeStruct((b, keep, width), F32))
        out_specs.append(pl.BlockSpec((1, kb, width), lambda bi, i, first=first: (bi, jnp.maximum(i - first, 0), 0)))
    kern = functools.partial(_proj_kernel, tm=tm, chunks=chunks, f32_outs=f32_outs, n_bf16=n_bf16)
    tab_spec = pl.BlockSpec((tm, LANES), lambda bi, i: (i, 0))
    return pl.pallas_call(
        kern,
        grid=(b, n_i),
        in_specs=[
            pl.BlockSpec((1, tm, d), lambda bi, i: (bi, i, 0)),
            _const_spec((1, d)),
            _const_spec(w.shape),
            tab_spec, tab_spec, tab_spec,
        ],
        out_specs=out_specs,
        out_shape=out_shapes,
        scratch_shapes=[pltpu.VMEM((d // LANES, tm, LANES), F32)],
        compiler_params=_params(("arbitrary", "arbitrary")),
        name="proj_rope",
    )(h, g, w, *tables)


def _attn_kernel(*refs):
    q_refs = refs[0:3]
    k_refs = refs[3:6]
    v_refs = refs[6:9]
    kp_refs = refs[9:12]
    vp_refs = refs[12:15]
    out_ref = refs[15]
    kb_refs = refs[16:19]
    vb_refs = refs[19:22]
    acc_ref, m_ref, l_ref = refs[22:25]
    i = pl.program_id(1)
    blk = SUB_BLOCK

    lane = lax.broadcasted_iota(jnp.int32, (blk, LANES), 1)
    first_head = lane < HEAD_DIM
    a = lax.broadcasted_iota(jnp.int32, (blk, 2 * blk), 0)
    c = lax.broadcasted_iota(jnp.int32, (blk, 2 * blk), 1)
    band = (c >= a) & (c <= a + blk)

    for g, (_, dil) in enumerate(ATTN_PATTERNS):
        nsub = ATTN_TILE // (dil * blk)
        kb, vb = kb_refs[g], vb_refs[g]
        kb[:, 0:blk, :] = kp_refs[g][0]
        kb[:, blk:, :] = k_refs[g][0]
        vb[:, 0:blk, :] = vp_refs[g][0]
        vb[:, blk:, :] = v_refs[g][0]

        def body(j, carry, g=g, dil=dil, nsub=nsub, kb=kb, vb=vb):
            r = j // nsub
            sub = j % nsub
            row0 = pl.multiple_of(sub * blk, blk)
            qb = q_refs[g][0, r, pl.ds(row0, blk), :]
            kcat = kb[r, pl.ds(row0, 2 * blk), :]
            vcat = vb[r, pl.ds(row0, 2 * blk), :]
            first_key = jnp.where((sub > 0) | (i > 0), 0, blk)
            valid = band & (c >= first_key)
            parts = []
            for hh in range(2):
                sel = first_head if hh == 0 else jnp.logical_not(first_head)
                qm = jnp.where(sel, qb, jnp.zeros_like(qb))
                s = lax.dot_general(qm, kcat, (((1,), (1,)), ((), ())), preferred_element_type=F32)
                s = jnp.where(valid, s, -jnp.inf)
                m = jnp.max(s, axis=-1, keepdims=True)
                p = jnp.exp(s - m)
                l = jnp.sum(p, axis=-1, keepdims=True)
                pv = jnp.dot(p.astype(BF16), vcat, preferred_element_type=F32)
                parts.append((pv, m, l))
            (pv0, m0, l0), (pv1, m1, l1) = parts
            rows = pl.ds(sub * (blk * dil) + r, blk, stride=dil) if dil > 1 else pl.ds(row0, blk)
            acc_ref[g, rows, :] = jnp.where(first_head, pv0, pv1)
            m_ref[g, rows, :] = jnp.where(first_head, m0, m1)
            l_ref[g, rows, :] = jnp.where(first_head, l0, l1)
            return carry

        lax.fori_loop(0, ATTN_TILE // blk, body, 0)

    def merge(t, carry):
        rows = pl.ds(pl.multiple_of(t * 256, 256), 256)
        ms = [m_ref[g, rows, :] for g in range(N_GROUPS)]
        mx = jnp.maximum(jnp.maximum(ms[0], ms[1]), ms[2])
        num = jnp.zeros((256, LANES), F32)
        den = jnp.zeros((256, LANES), F32)
        for g in range(N_GROUPS):
            e = jnp.exp(ms[g] - mx)
            num = num + e * acc_ref[g, rows, :]
            den = den + e * l_ref[g, rows, :]
        out_ref[0, rows, :] = num / den
        return carry

    lax.fori_loop(0, ATTN_TILE // 256, merge, 0)


def _attention(qs, ks, vs):
    b = qs[0].shape[0]
    s = qs[0].shape[1] * qs[0].shape[2]
    n_i = s // ATTN_TILE
    n_p = D_MODEL // LANES
    blk = SUB_BLOCK
    cur_specs, prev_specs, scratch = [], [], []
    for _, dil in ATTN_PATTERNS:
        per = ATTN_TILE // dil
        cur_specs.append(pl.BlockSpec((1, dil, per, LANES), lambda bi, i, p: (bi, 0, i, p)))
        nsub = per // blk
        prev_specs.append(pl.BlockSpec((1, dil, blk, LANES),
                                       lambda bi, i, p, nsub=nsub: (bi, 0, jnp.maximum(i * nsub - 1, 0), p)))
        scratch.append(pltpu.VMEM((dil, blk + per, LANES), BF16))
    return pl.pallas_call(
        _attn_kernel,
        grid=(b, n_i, n_p),
        in_specs=cur_specs * 3 + prev_specs * 2,
        out_specs=pl.BlockSpec((1, ATTN_TILE, LANES), lambda bi, i, p: (bi, i, p)),
        out_shape=jax.ShapeDtypeStruct((b, s, D_MODEL), F32),
        scratch_shapes=scratch * 2 + [pltpu.VMEM((N_GROUPS, ATTN_TILE, LANES), F32)] * 3,
        compiler_params=_params(("arbitrary", "arbitrary", "arbitrary")),
        name="dilated_attention",
    )(*qs, *ks, *vs, *ks, *vs)


def _outproj_kernel(h_ref, comb_ref, g_ref, wg_ref, wo_ref, gf_ref, out_ref, *, final):
    x = h_ref[0]
    hn = _rmsnorm(x, g_ref[...]).astype(BF16)
    gate = jnp.dot(hn, wg_ref[...], preferred_element_type=F32)
    y = comb_ref[0] * _silu(gate)
    o = x + jnp.dot(y.astype(BF16), wo_ref[...], preferred_element_type=F32)
    if final:
        o = _rmsnorm(o, gf_ref[...])
    out_ref[0] = o


def _outproj(h, comb, g, w_gate, w_out, g_final, *, final, tm):
    b, s, d = h.shape
    kern = functools.partial(_outproj_kernel, final=final)
    tile = pl.BlockSpec((1, tm, d), lambda bi, i: (bi, i, 0))
    return pl.pallas_call(
        kern,
        grid=(b, s // tm),
        in_specs=[tile, tile, _const_spec((1, d)), _const_spec((d, d)), _const_spec((d, d)), _const_spec((1, d))],
        out_specs=tile,
        out_shape=jax.ShapeDtypeStruct((b, s, d), F32),
        compiler_params=_params(("arbitrary", "arbitrary")),
        name="gate_outproj",
    )(h, comb, g, w_gate, w_out, g_final)


def _sattn_kernel(q_ref, c_ref, seg_ref, segt_ref, acc_ref, m_ref, l_ref, *, bb):
    n = c_ref.shape[1]
    q = q_ref[...]
    k = c_ref[:, :, 0:D_MODEL]
    v = c_ref[:, :, D_MODEL:]
    prod = (k * q[:, None, :]).reshape(bb * n, D_MODEL).astype(BF16)
    s = jnp.dot(prod, seg_ref[...], preferred_element_type=F32).reshape(bb, n, LANES)
    m = jnp.max(s, axis=1)
    p = jnp.exp(s - m[:, None, :])
    l = jnp.sum(p, axis=1)
    pe = jnp.dot(p.reshape(bb * n, LANES).astype(BF16), segt_ref[...], preferred_element_type=F32)
    acc_ref[...] = jnp.sum(pe.reshape(bb, n, D_MODEL) * v, axis=1)
    m_ref[...] = m
    l_ref[...] = l


def _sample_attention(q, cache3d, seg, segt, *, bb):
    n_b, n_rows, _ = cache3d.shape
    kern = functools.partial(_sattn_kernel, bb=bb)
    return pl.pallas_call(
        kern,
        grid=(n_b // bb,),
        in_specs=[
            pl.BlockSpec((bb, D_MODEL), lambda i: (i, 0)),
            pl.BlockSpec((bb, SUB_BLOCK, 2 * D_MODEL), lambda i: (i, 0, 0)),
            _const_spec(seg.shape),
            _const_spec(segt.shape),
        ],
        out_specs=[
            pl.BlockSpec((bb, D_MODEL), lambda i: (i, 0)),
            pl.BlockSpec((bb, LANES), lambda i: (i, 0)),
            pl.BlockSpec((bb, LANES), lambda i: (i, 0)),
        ],
        out_shape=[
            jax.ShapeDtypeStruct((n_b, D_MODEL), F32),
            jax.ShapeDtypeStruct((n_b, LANES), F32),
            jax.ShapeDtypeStruct((n_b, LANES), F32),
        ],
        compiler_params=_params(("arbitrary",)),
        name="sample_attention",
    )(q, cache3d, seg, segt)


def _expand_heads(x, segt):
    x1 = x.astype(BF16)
    r1 = x - x1.astype(F32)
    x2 = r1.astype(BF16)
    x3 = (r1 - x2.astype(F32)).astype(BF16)
    out = jnp.dot(x1, segt, preferred_element_type=F32)
    out = out + jnp.dot(x2, segt, preferred_element_type=F32)
    return out + jnp.dot(x3, segt, preferred_element_type=F32)


def _soutproj_kernel(h_ref, q_ref, kvn0_ref, kvn1_ref, kvn2_ref, acc_ref, m_ref, l_ref, seg_ref, segt_ref,
                     g_ref, wg_ref, wo_ref, gf_ref, out_ref, *, final):
    kvn_refs = (kvn0_ref, kvn1_ref, kvn2_ref)
    seg, segt = seg_ref[...], segt_ref[...]
    s_new = []
    for g in range(N_GROUPS):
        qg = q_ref[:, g * D_MODEL:(g + 1) * D_MODEL]
        kn = kvn_refs[g][:, 0:D_MODEL]
        s_new.append(jnp.dot((qg * kn).astype(BF16), seg, preferred_element_type=F32))
    ms = [m_ref[g] for g in range(N_GROUPS)]
    mx = ms[0]
    for t in ms[1:] + s_new:
        mx = jnp.maximum(mx, t)
    e_old = [jnp.exp(ms[g] - mx) for g in range(N_GROUPS)]
    e_new = [jnp.exp(s_new[g] - mx) for g in range(N_GROUPS)]
    den = jnp.zeros_like(mx)
    for g in range(N_GROUPS):
        den = den + e_old[g] * l_ref[g] + e_new[g]
    comb = jnp.zeros(h_ref.shape, F32)
    for g in range(N_GROUPS):
        comb = comb + _expand_heads(e_old[g] / den, segt) * acc_ref[g]
        comb = comb + _expand_heads(e_new[g] / den, segt) * kvn_refs[g][:, D_MODEL:]

    x = h_ref[...]
    hn = _rmsnorm(x, g_ref[...]).astype(BF16)
    gate = jnp.dot(hn, wg_ref[...], preferred_element_type=F32)
    y = comb * _silu(gate)
    o = x + jnp.dot(y.astype(BF16), wo_ref[...], preferred_element_type=F32)
    if final:
        o = _rmsnorm(o, gf_ref[...])
    out_ref[...] = o


def _sample_outproj(h, q, kvn, acc, m, l, seg, segt, g, w_gate, w_out, g_final, *, final):
    kern = functools.partial(_soutproj_kernel, final=final)
    return pl.pallas_call(
        kern,
        out_shape=jax.ShapeDtypeStruct(h.shape, F32),
        compiler_params=pltpu.CompilerParams(vmem_limit_bytes=VMEM_LIMIT),
        name="gate_outproj_sample",
    )(h, q, *kvn, acc, m, l, seg, segt, g, w_gate, w_out, g_final)


def kernel(x_prompt, x_sample, state_pool, cache_kv_w128, cache_kv_w512, cache_kv_w2048, g_a, w_a_in, w_a_group,
           a_scale, w_a_out, g_kv, w_kv, g_b, w_b_in, w_b_out, g_final):
    b, s, d = x_prompt.shape
    n_s, t_s, _ = x_sample.shape
    assert d == D_MODEL and t_s == 1 and s % ATTN_TILE == 0
    n_pool = w_a_in.shape[0]
    n_attn = w_b_in.shape[0]
    caches = (cache_kv_w128, cache_kv_w512, cache_kv_w2048)
    dils = tuple(dl for _, dl in ATTN_PATTERNS)
    keeps = tuple(min(w, s) for w, _ in ATTN_PATTERNS)
    tm = 256

    wb = lambda a: a.astype(BF16)
    row = lambda a: a.reshape(1, -1)
    w_a_in_b, w_a_group_b, w_a_out_b = wb(w_a_in), wb(w_a_group), wb(w_a_out)
    w_kv_b, w_b_in_b, w_b_out_b = wb(w_kv), wb(w_b_in), wb(w_b_out)
    q_cols = N_GROUPS * D_MODEL

    tab_p = _rope_tables(jnp.arange(s, dtype=jnp.int32))
    tab_s = _rope_tables(jnp.broadcast_to(PAST_LEN + jnp.arange(t_s, dtype=jnp.int32), (n_s,)))
    head_of_lane = jnp.arange(D_MODEL, dtype=jnp.int32) // HEAD_DIM
    seg = (head_of_lane[:, None] == jnp.arange(LANES, dtype=jnp.int32)[None, :]).astype(BF16)
    segt = seg.T

    h_p = x_prompt
    h_s = x_sample.reshape(n_s, d)
    pool_p, pool_s = [], []
    for li in range(n_pool):
        h_p, ulast = _pool_layer(h_p, row(g_a[li]), w_a_in_b[li], w_a_group_b[li], row(a_scale[li]), w_a_out_b[li], tm=tm)
        pool_p.append(ulast[:, 16 - POOL_STATE:])
        h_s, nst = _spool_layer(h_s, state_pool[li].reshape(n_s, POOL_STATE * d), row(g_a[li]), w_a_in_b[li],
                                w_a_group_b[li], row(a_scale[li]), w_a_out_b[li])
        pool_s.append(nst.reshape(n_s, POOL_STATE, d))

    kv_chunks = []
    for gi in range(N_GROUPS):
        kv_chunks.append((True, 1.0, dils[gi], True, (gi, 0)))
        kv_chunks.append((False, 1.0, dils[gi], True, (gi, D_MODEL)))
    kv_out = _proj(h_p, row(g_kv), w_kv_b, tab_p, tuple(kv_chunks), tuple((kp, 2 * D_MODEL) for kp in keeps), tm=tm)
    ks = [kv_out[2 * gi] for gi in range(N_GROUPS)]
    vs = [kv_out[2 * gi + 1] for gi in range(N_GROUPS)]
    kv_keep_p = kv_out[2 * N_GROUPS:]

    skv_chunks = []
    for gi in range(N_GROUPS):
        skv_chunks.append((True, 1.0, 1, False, (gi, 0)))
        skv_chunks.append((False, 1.0, 1, False, (gi, D_MODEL)))
    kvn = _proj(h_s[None], row(g_kv), w_kv_b, tab_s, tuple(skv_chunks), ((n_s, 2 * D_MODEL),) * N_GROUPS, tm=n_s)
    kvn = [a[0] for a in kvn]

    q_scale = 1.0 / math.sqrt(HEAD_DIM)
    q_chunks = tuple((True, q_scale, dils[gi], True, None) for gi in range(N_GROUPS))
    sq_chunks = tuple((True, q_scale, 1, False, (0, gi * D_MODEL)) for gi in range(N_GROUPS))
    cache3d = [c.reshape(n_s, SUB_BLOCK, dl * 2 * D_MODEL) for c, dl in zip(caches, dils)]

    for lj in range(n_attn):
        final = lj == n_attn - 1
        w_q = w_b_in_b[lj][:, :q_cols]
        w_gate = w_b_in_b[lj][:, q_cols:]
        qs = _proj(h_p, row(g_b[lj]), w_q, tab_p, q_chunks, (), tm=tm)
        comb = _attention(qs, ks, vs)
        h_p = _outproj(h_p, comb, row(g_b[lj]), w_gate, w_b_out_b[lj], row(g_final), final=final, tm=tm)

        (q_s,) = _proj(h_s[None], row(g_b[lj]), w_q, tab_s, sq_chunks, ((n_s, q_cols),), tm=n_s)
        q_s = q_s[0]
        parts = [_sample_attention(q_s[:, gi * D_MODEL:(gi + 1) * D_MODEL], cache3d[gi], seg, segt, bb=8)
                 for gi in range(N_GROUPS)]
        acc = jnp.stack([pt[0] for pt in parts])
        m = jnp.stack([pt[1] for pt in parts])
        l = jnp.stack([pt[2] for pt in parts])
        h_s = _sample_outproj(h_s, q_s, kvn, acc, m, l, seg, segt, row(g_b[lj]), w_gate, w_b_out_b[lj],
                              row(g_final), final=final)

    kv_shape = (2, N_HEADS, HEAD_DIM)
    outs = [h_p, h_s.reshape(n_s, t_s, d), jnp.stack(pool_p), jnp.stack(pool_s)]
    for gi in range(N_GROUPS):
        outs.append(kv_keep_p[gi].reshape(b, keeps[gi], *kv_shape))
        outs.append(kvn[gi].reshape(n_s, t_s, *kv_shape))
    return tuple(outs)
```

```python
import functools
import math

import jax
import jax.numpy as jnp
from jax import lax
from jax.experimental import pallas as pl
from jax.experimental.pallas import tpu as pltpu

D_MODEL = 1024
HEAD_DIM = 64
N_HEADS = D_MODEL // HEAD_DIM
ROT_DIM = HEAD_DIM // 4
ROPE_THETA = 500000.0
POOL_WINDOWS = (2, 4, 8, 16)
POOL_GROUP_WIDTH = D_MODEL // len(POOL_WINDOWS)
POOL_STATE = max(POOL_WINDOWS) - 1
ATTN_PATTERNS = ((128, 1), (512, 4), (2048, 16))
N_GROUPS = len(ATTN_PATTERNS)
SUB_BLOCK = 128
RMS_EPS = 1e-6
PAST_LEN = 2048

LANES = 128
ATTN_TILE = SUB_BLOCK * max(d for _, d in ATTN_PATTERNS)
POOL_HIST = 32
VMEM_LIMIT = 56 * 1024 * 1024

F32 = jnp.float32
BF16 = jnp.bfloat16


def _rmsnorm(x, g):
    ms = jnp.mean(x * x, axis=-1, keepdims=True)
    return x * lax.rsqrt(ms + RMS_EPS) * g


def _silu(x):
    return x * (1.0 / (1.0 + jnp.exp(-x)))


def _const_spec(shape):
    return pl.BlockSpec(shape, lambda *_: (0,) * len(shape), pipeline_mode=pl.Buffered(1))


def _params(sem):
    return pltpu.CompilerParams(dimension_semantics=sem, vmem_limit_bytes=VMEM_LIMIT)


def _pool_kernel(h_ref, g_ref, win_ref, wgrp_ref, scale_ref, wout_ref, out_ref, ulast_ref,
                 ext_ref, lvl_ref, *, tm):
    i = pl.program_id(1)
    n_i = pl.num_programs(1)
    hist = POOL_HIST

    @pl.when(i == 0)
    def _():
        ext_ref[0:hist, :] = jnp.zeros((hist, D_MODEL), F32)

    x = h_ref[0]
    hn = _rmsnorm(x, g_ref[...]).astype(BF16)
    proj = jnp.dot(hn, win_ref[...], preferred_element_type=F32)
    gate = proj[:, D_MODEL:]
    ext_ref[hist:hist + tm, :] = proj[:, :D_MODEL]

    total = hist + tm
    pos = i * tm + lax.broadcasted_iota(jnp.int32, (tm, 1), 0)
    src = ext_ref
    z_parts = []
    for k, w in enumerate(POOL_WINDOWS):
        shift = w // 2
        lo = 8 * (k + 1)
        c0 = k * POOL_GROUP_WIDTH
        cur = src[lo:total, c0:] + src[lo - shift:total - shift, c0:]
        dst = lvl_ref.at[k % 2]
        dst[lo:total, c0:] = cur
        inv_cnt = 1.0 / jnp.minimum(pos + 1, w).astype(F32)
        wsum = dst[hist:total, c0:c0 + POOL_GROUP_WIDTH]
        r = wsum * inv_cnt - ext_ref[hist:total, c0:c0 + POOL_GROUP_WIDTH]
        z_parts.append(jnp.dot(r.astype(BF16), wgrp_ref[k], preferred_element_type=F32))
        src = dst
    z = jnp.concatenate(z_parts, axis=-1)
    y = z * scale_ref[...] * _silu(gate)
    out_ref[0] = x + jnp.dot(y.astype(BF16), wout_ref[...], preferred_element_type=F32)

    @pl.when(i == n_i - 1)
    def _():
        ulast_ref[0] = ext_ref[hist + tm - 16:hist + tm, :]

    ext_ref[0:hist, :] = ext_ref[tm:tm + hist, :]


def _pool_layer(h, g, w_in, w_grp, scale, w_out, *, tm):
    b, s, d = h.shape
    kern = functools.partial(_pool_kernel, tm=tm)
    return pl.pallas_call(
        kern,
        grid=(b, s // tm),
        in_specs=[
            pl.BlockSpec((1, tm, d), lambda bi, i: (bi, i, 0)),
            _const_spec((1, d)),
            _const_spec((d, 2 * d)),
            _const_spec((len(POOL_WINDOWS), POOL_GROUP_WIDTH, POOL_GROUP_WIDTH)),
            _const_spec((1, d)),
            _const_spec((d, d)),
        ],
        out_specs=[
            pl.BlockSpec((1, tm, d), lambda bi, i: (bi, i, 0)),
            pl.BlockSpec((1, 16, d), lambda bi, i: (bi, 0, 0)),
        ],
        out_shape=[jax.ShapeDtypeStruct((b, s, d), F32), jax.ShapeDtypeStruct((b, 16, d), F32)],
        scratch_shapes=[pltpu.VMEM((POOL_HIST + tm, d), F32), pltpu.VMEM((2, POOL_HIST + tm, d), F32)],
        compiler_params=_params(("arbitrary", "arbitrary")),
        name="pool_layer",
    )(h, g, w_in, w_grp, scale, w_out)


def _spool_kernel(h_ref, st_ref, g_ref, win_ref, wgrp_ref, scale_ref, wout_ref, out_ref, nst_ref):
    x = h_ref[...]
    hn = _rmsnorm(x, g_ref[...]).astype(BF16)
    proj = jnp.dot(hn, win_ref[...], preferred_element_type=F32)
    u = proj[:, :D_MODEL]
    gate = proj[:, D_MODEL:]
    z_parts = []
    for k, w in enumerate(POOL_WINDOWS):
        c0 = k * POOL_GROUP_WIDTH
        uk = u[:, c0:c0 + POOL_GROUP_WIDTH]
        tot = uk
        for j in range(1, w):
            tot = tot + st_ref[POOL_STATE - j, :, c0:c0 + POOL_GROUP_WIDTH]
        cnt = float(min(PAST_LEN + 1, w))
        r = tot / cnt - uk
        z_parts.append(jnp.dot(r.astype(BF16), wgrp_ref[k], preferred_element_type=F32))
    z = jnp.concatenate(z_parts, axis=-1)
    y = z * scale_ref[...] * _silu(gate)
    out_ref[...] = x + jnp.dot(y.astype(BF16), wout_ref[...], preferred_element_type=F32)
    nst_ref[0:POOL_STATE - 1] = st_ref[1:POOL_STATE]
    nst_ref[POOL_STATE - 1] = u


def _spool_layer(h, state_t, g, w_in, w_grp, scale, w_out):
    n, d = h.shape
    return pl.pallas_call(
        _spool_kernel,
        out_shape=[jax.ShapeDtypeStruct((n, d), F32), jax.ShapeDtypeStruct(state_t.shape, F32)],
        compiler_params=pltpu.CompilerParams(vmem_limit_bytes=VMEM_LIMIT),
        name="pool_layer_sample",
    )(h, state_t, g, w_in, w_grp, scale, w_out)


def _rope_tables(pos):
    half = ROT_DIM // 2
    inv = 1.0 / (ROPE_THETA ** (jnp.arange(0, ROT_DIM, 2, dtype=jnp.float32) / ROT_DIM))
    ang = pos.astype(jnp.float32)[:, None] * inv[None, :]
    cos, sin = jnp.cos(ang), jnp.sin(ang)
    t = pos.shape[0]
    rest = HEAD_DIM - ROT_DIM
    cos_h = jnp.concatenate([cos, cos, jnp.ones((t, rest), F32)], axis=1)
    sin_a = jnp.concatenate([jnp.zeros((t, half), F32), sin, jnp.zeros((t, rest), F32)], axis=1)
    sin_b = jnp.concatenate([-sin, jnp.zeros((t, half + rest), F32)], axis=1)
    reps = LANES // HEAD_DIM
    return tuple(jnp.tile(a, (1, reps)) for a in (cos_h, sin_a, sin_b))


def _proj_kernel(h_ref, g_ref, w_ref, cos_ref, sa_ref, sb_ref, *rest, tm, chunks, f32_outs, n_bf16):
    bf_refs = rest[:n_bf16]
    f32_refs = rest[n_bf16:n_bf16 + len(f32_outs)]
    y_ref = rest[-1]
    i = pl.program_id(1)
    n_i = pl.num_programs(1)
    half = ROT_DIM // 2

    hn = _rmsnorm(h_ref[0], g_ref[...]).astype(BF16)
    bi = 0
    n_lt = D_MODEL // LANES
    for c, (rope, scale, dil, bf_out, f32_slot) in enumerate(chunks):
        y = jnp.dot(hn, w_ref[:, c * D_MODEL:(c + 1) * D_MODEL], preferred_element_type=F32)
        for lt in range(n_lt):
            ys = y[:, lt * LANES:(lt + 1) * LANES]
            if rope:
                ys = (ys * cos_ref[...] + pltpu.roll(ys, half, 1) * sa_ref[...]
                      + pltpu.roll(ys, LANES - half, 1) * sb_ref[...])
            y_ref[lt] = ys * scale if scale != 1.0 else ys
        if f32_slot is not None:
            oi, off = f32_slot
            keep = f32_outs[oi][0]
            kb = min(tm, keep)
            first = n_i - keep // kb

            @pl.when(i >= first)
            def _(oi=oi, off=off, kb=kb):
                for lt in range(n_lt):
                    f32_refs[oi][0, :, off + lt * LANES:off + (lt + 1) * LANES] = y_ref[lt, tm - kb:tm, :]
        if bf_out:
            o_ref = bf_refs[bi]
            bi += 1
            for lt in range(n_lt):
                cols = slice(lt * LANES, (lt + 1) * LANES)
                if dil == 1:
                    o_ref[0, 0, :, cols] = y_ref[lt].astype(BF16)
                else:
                    for r in range(dil):
                        o_ref[0, r, :, cols] = y_ref[lt, pl.ds(r, tm // dil, stride=dil), :].astype(BF16)


def _proj(h, g, w, tables, chunks, f32_outs, *, tm):
    b, s, d = h.shape
    n_i = s // tm
    out_shapes, out_specs = [], []
    for rope, scale, dil, bf_out, f32_slot in chunks:
        if bf_out:
            out_shapes.append(jax.ShapeDtypeStruct((b, dil, s // dil, d), BF16))
            out_specs.append(pl.BlockSpec((1, dil, tm // dil, d), lambda bi, i: (bi, 0, i, 0)))
    n_bf16 = len(out_shapes)
    for keep, width in f32_outs:
        kb = min(tm, keep)
        first = n_i - keep // kb
        out_shapes.append(jax.ShapeDtypeStruct((b, keep, width), F32))
        out_specs.append(pl.BlockSpec((1, kb, width), lambda bi, i, first=first: (bi, jnp.maximum(i - first, 0), 0)))
    kern = functools.partial(_proj_kernel, tm=tm, chunks=chunks, f32_outs=f32_outs, n_bf16=n_bf16)
    tab_spec = pl.BlockSpec((tm, LANES), lambda bi, i: (i, 0))
    return pl.pallas_call(
        kern,
        grid=(b, n_i),
        in_specs=[
            pl.BlockSpec((1, tm, d), lambda bi, i: (bi, i, 0)),
            _const_spec((1, d)),
            _const_spec(w.shape),
            tab_spec, tab_spec, tab_spec,
        ],
        out_specs=out_specs,
        out_shape=out_shapes,
        scratch_shapes=[pltpu.VMEM((d // LANES, tm, LANES), F32)],
        compiler_params=_params(("arbitrary", "arbitrary")),
        name="proj_rope",
    )(h, g, w, *tables)


def _attn_kernel(*refs):
    q_refs = refs[0:3]
    k_refs = refs[3:6]
    v_refs = refs[6:9]
    kp_refs = refs[9:12]
    vp_refs = refs[12:15]
    out_ref = refs[15]
    acc_ref, m_ref, l_ref = refs[16:19]
    i = pl.program_id(1)
    blk = SUB_BLOCK

    lane_q = lax.broadcasted_iota(jnp.int32, (blk, LANES), 1) < HEAD_DIM
    key = lax.broadcasted_iota(jnp.int32, (blk, 2 * blk), 1)
    qry = lax.broadcasted_iota(jnp.int32, (blk, 2 * blk), 0)
    band = (key >= qry) & (key <= qry + blk)
    band_first = band & (key >= jnp.where(i > 0, 0, blk))
    ones_v = jnp.ones((2 * blk, LANES), BF16)

    def sub_block(g, dil, r, sub):
        rows_q = slice(sub * blk, (sub + 1) * blk)
        qb = q_refs[g][0, r, rows_q, :]
        if sub == 0:
            kcat = jnp.concatenate([kp_refs[g][0, r], k_refs[g][0, r, 0:blk, :]], axis=0)
            vcat = jnp.concatenate([vp_refs[g][0, r], v_refs[g][0, r, 0:blk, :]], axis=0)
            valid = band_first
        else:
            kcat = k_refs[g][0, r, (sub - 1) * blk:(sub + 1) * blk, :]
            vcat = v_refs[g][0, r, (sub - 1) * blk:(sub + 1) * blk, :]
            valid = band
        parts = []
        for first in (True, False):
            qm = jnp.where(lane_q, qb, jnp.zeros_like(qb)) if first else jnp.where(lane_q, jnp.zeros_like(qb), qb)
            s = lax.dot_general(qm, kcat, (((1,), (1,)), ((), ())), preferred_element_type=F32)
            s = jnp.where(valid, s, -jnp.inf)
            m = jnp.max(s, axis=-1, keepdims=True)
            p = jnp.exp(s - m).astype(BF16)
            pv = jnp.dot(p, jnp.concatenate([vcat, ones_v], axis=1), preferred_element_type=F32)
            parts.append((pv, m))
        (pv0, m0), (pv1, m1) = parts
        rows = pl.ds(pl.multiple_of(r * (ATTN_TILE // dil) + sub * blk, blk), blk)
        acc_ref[g, rows, :] = jnp.where(lane_q, pv0[:, :LANES], pv1[:, :LANES])
        l_ref[g, rows, :] = jnp.where(lane_q, pv0[:, LANES:], pv1[:, LANES:])
        m_ref[g, rows, :] = jnp.where(lane_q, m0, m1)

    for g, (_, dil) in enumerate(ATTN_PATTERNS):
        nsub = ATTN_TILE // (dil * blk)

        def residue(r, carry, g=g, dil=dil, nsub=nsub):
            for sub in range(nsub):
                sub_block(g, dil, r, sub)
            return carry

        if dil == 1:
            residue(0, 0)
        else:
            lax.fori_loop(0, dil, residue, 0, unroll=min(dil, max(1, 8 // nsub)))

    big = ATTN_TILE // blk

    def merge(rb, carry):
        num = jnp.zeros((blk, LANES), F32)
        den = jnp.zeros((blk, LANES), F32)
        rows_g = []
        for g, (_, dil) in enumerate(ATTN_PATTERNS):
            step = big // dil
            start = (rb % dil) * (ATTN_TILE // dil) + rb // dil
            rows_g.append(pl.ds(start, blk, stride=step) if step > 1 else pl.ds(pl.multiple_of(start, blk), blk))
        ms = [m_ref[g, rows_g[g], :] for g in range(N_GROUPS)]
        mx = jnp.maximum(jnp.maximum(ms[0], ms[1]), ms[2])
        for g in range(N_GROUPS):
            e = jnp.exp(ms[g] - mx)
            num = num + e * acc_ref[g, rows_g[g], :]
            den = den + e * l_ref[g, rows_g[g], :]
        out_ref[0, pl.ds(rb, blk, stride=big), :] = num / den
        return carry

    lax.fori_loop(0, big, merge, 0, unroll=4)


def _attention(qs, ks, vs):
    b = qs[0].shape[0]
    s = qs[0].shape[1] * qs[0].shape[2]
    n_i = s // ATTN_TILE
    n_p = D_MODEL // LANES
    blk = SUB_BLOCK
    cur_specs, prev_specs = [], []
    for _, dil in ATTN_PATTERNS:
        per = ATTN_TILE // dil
        cur_specs.append(pl.BlockSpec((1, dil, per, LANES), lambda bi, i, p: (bi, 0, i, p)))
        nsub = per // blk
        prev_specs.append(pl.BlockSpec((1, dil, blk, LANES),
                                       lambda bi, i, p, nsub=nsub: (bi, 0, jnp.maximum(i * nsub - 1, 0), p)))
    return pl.pallas_call(
        _attn_kernel,
        grid=(b, n_i, n_p),
        in_specs=cur_specs * 3 + prev_specs * 2,
        out_specs=pl.BlockSpec((1, ATTN_TILE, LANES), lambda bi, i, p: (bi, i, p)),
        out_shape=jax.ShapeDtypeStruct((b, s, D_MODEL), F32),
        scratch_shapes=[pltpu.VMEM((N_GROUPS, ATTN_TILE, LANES), F32)] * 3,
        compiler_params=_params(("arbitrary", "arbitrary", "arbitrary")),
        name="dilated_attention",
    )(*qs, *ks, *vs, *ks, *vs)


def _outproj_kernel(h_ref, comb_ref, g_ref, wg_ref, wo_ref, gf_ref, out_ref, *, final):
    x = h_ref[0]
    hn = _rmsnorm(x, g_ref[...]).astype(BF16)
    gate = jnp.dot(hn, wg_ref[...], preferred_element_type=F32)
    y = comb_ref[0] * _silu(gate)
    o = x + jnp.dot(y.astype(BF16), wo_ref[...], preferred_element_type=F32)
    if final:
        o = _rmsnorm(o, gf_ref[...])
    out_ref[0] = o


def _outproj(h, comb, g, w_gate, w_out, g_final, *, final, tm):
    b, s, d = h.shape
    kern = functools.partial(_outproj_kernel, final=final)
    tile = pl.BlockSpec((1, tm, d), lambda bi, i: (bi, i, 0))
    return pl.pallas_call(
        kern,
        grid=(b, s // tm),
        in_specs=[tile, tile, _const_spec((1, d)), _const_spec((d, d)), _const_spec((d, d)), _const_spec((1, d))],
        out_specs=tile,
        out_shape=jax.ShapeDtypeStruct((b, s, d), F32),
        compiler_params=_params(("arbitrary", "arbitrary")),
        name="gate_outproj",
    )(h, comb, g, w_gate, w_out, g_final)


def _gather_kernel(x_ref, o_ref, t_ref, *, dil):
    n_pos = x_ref.shape[-1]
    pair = LANES // HEAD_DIM
    for p in range(N_HEADS // pair):
        x = x_ref[0, 0, pair * p:pair * (p + 1)].reshape(LANES, n_pos)
        slot = p % 2
        t_ref[slot] = x.T
        if dil > 1:
            kept = t_ref[slot, pl.ds(0, n_pos // dil, stride=dil), :]
        else:
            kept = t_ref[slot]
        o_ref[0, :, p * LANES:(p + 1) * LANES] = kept.astype(BF16)


def _gather_cache(cache_t, dil):
    n_b, n_kv, n_h, hd, n_pos = cache_t.shape
    kept = n_pos // dil
    kern = functools.partial(_gather_kernel, dil=dil)
    return pl.pallas_call(
        kern,
        grid=(n_b, n_kv),
        in_specs=[pl.BlockSpec((1, 1, n_h, hd, n_pos), lambda bi, kv: (bi, kv, 0, 0, 0))],
        out_specs=pl.BlockSpec((1, kept, n_h * hd), lambda bi, kv: (bi, 0, kv)),
        out_shape=jax.ShapeDtypeStruct((n_b, kept, n_kv * n_h * hd), BF16),
        scratch_shapes=[pltpu.VMEM((2, n_pos, LANES), F32)],
        compiler_params=_params(("arbitrary", "arbitrary")),
        name="gather_cache",
    )(cache_t)


def _sattn_kernel(q_ref, c_ref, seg_ref, segt_ref, acc_ref, m_ref, l_ref, *, bb):
    n = c_ref.shape[1]
    q = q_ref[...]
    k = c_ref[:, :, 0:D_MODEL].astype(F32)
    v = c_ref[:, :, D_MODEL:].astype(F32)
    prod = (k * q[:, None, :]).reshape(bb * n, D_MODEL).astype(BF16)
    s = jnp.dot(prod, seg_ref[...], preferred_element_type=F32).reshape(bb, n, LANES)
    m = jnp.max(s, axis=1)
    p = jnp.exp(s - m[:, None, :])
    l = jnp.sum(p, axis=1)
    pe = jnp.dot(p.reshape(bb * n, LANES).astype(BF16), segt_ref[...], preferred_element_type=F32)
    acc_ref[...] = jnp.sum(pe.reshape(bb, n, D_MODEL) * v, axis=1)
    m_ref[...] = m
    l_ref[...] = l


def _sample_attention(q, rows, seg, segt, *, bb):
    n_b, n_rows, width = rows.shape
    kern = functools.partial(_sattn_kernel, bb=bb)
    return pl.pallas_call(
        kern,
        grid=(n_b // bb,),
        in_specs=[
            pl.BlockSpec((bb, D_MODEL), lambda i: (i, 0)),
            pl.BlockSpec((bb, n_rows, width), lambda i: (i, 0, 0)),
            _const_spec(seg.shape),
            _const_spec(segt.shape),
        ],
        out_specs=[
            pl.BlockSpec((bb, D_MODEL), lambda i: (i, 0)),
            pl.BlockSpec((bb, LANES), lambda i: (i, 0)),
            pl.BlockSpec((bb, LANES), lambda i: (i, 0)),
        ],
        out_shape=[
            jax.ShapeDtypeStruct((n_b, D_MODEL), F32),
            jax.ShapeDtypeStruct((n_b, LANES), F32),
            jax.ShapeDtypeStruct((n_b, LANES), F32),
        ],
        compiler_params=_params(("arbitrary",)),
        name="sample_attention",
    )(q, rows, seg, segt)


def _expand_heads(x, segt):
    x1 = x.astype(BF16)
    r1 = x - x1.astype(F32)
    x2 = r1.astype(BF16)
    x3 = (r1 - x2.astype(F32)).astype(BF16)
    out = jnp.dot(x1, segt, preferred_element_type=F32)
    out = out + jnp.dot(x2, segt, preferred_element_type=F32)
    return out + jnp.dot(x3, segt, preferred_element_type=F32)


def _soutproj_kernel(h_ref, q_ref, kvn0_ref, kvn1_ref, kvn2_ref, acc_ref, m_ref, l_ref, seg_ref, segt_ref,
                     g_ref, wg_ref, wo_ref, gf_ref, out_ref, *, final):
    kvn_refs = (kvn0_ref, kvn1_ref, kvn2_ref)
    seg, segt = seg_ref[...], segt_ref[...]
    s_new = []
    for g in range(N_GROUPS):
        qg = q_ref[:, g * D_MODEL:(g + 1) * D_MODEL]
        kn = kvn_refs[g][:, 0:D_MODEL]
        s_new.append(jnp.dot((qg * kn).astype(BF16), seg, preferred_element_type=F32))
    ms = [m_ref[g] for g in range(N_GROUPS)]
    mx = ms[0]
    for t in ms[1:] + s_new:
        mx = jnp.maximum(mx, t)
    e_old = [jnp.exp(ms[g] - mx) for g in range(N_GROUPS)]
    e_new = [jnp.exp(s_new[g] - mx) for g in range(N_GROUPS)]
    den = jnp.zeros_like(mx)
    for g in range(N_GROUPS):
        den = den + e_old[g] * l_ref[g] + e_new[g]
    comb = jnp.zeros(h_ref.shape, F32)
    for g in range(N_GROUPS):
        comb = comb + _expand_heads(e_old[g] / den, segt) * acc_ref[g]
        comb = comb + _expand_heads(e_new[g] / den, segt) * kvn_refs[g][:, D_MODEL:]

    x = h_ref[...]
    hn = _rmsnorm(x, g_ref[...]).astype(BF16)
    gate = jnp.dot(hn, wg_ref[...], preferred_element_type=F32)
    y = comb * _silu(gate)
    o = x + jnp.dot(y.astype(BF16), wo_ref[...], preferred_element_type=F32)
    if final:
        o = _rmsnorm(o, gf_ref[...])
    out_ref[...] = o


def _sample_outproj(h, q, kvn, acc, m, l, seg, segt, g, w_gate, w_out, g_final, *, final):
    kern = functools.partial(_soutproj_kernel, final=final)
    return pl.pallas_call(
        kern,
        out_shape=jax.ShapeDtypeStruct(h.shape, F32),
        compiler_params=pltpu.CompilerParams(vmem_limit_bytes=VMEM_LIMIT),
        name="gate_outproj_sample",
    )(h, q, *kvn, acc, m, l, seg, segt, g, w_gate, w_out, g_final)


def kernel(x_prompt, x_sample, state_pool, cache_kv_w128, cache_kv_w512, cache_kv_w2048, g_a, w_a_in, w_a_group,
           a_scale, w_a_out, g_kv, w_kv, g_b, w_b_in, w_b_out, g_final):
    b, s, d = x_prompt.shape
    n_s, t_s, _ = x_sample.shape
    assert d == D_MODEL and t_s == 1 and s % ATTN_TILE == 0
    n_pool = w_a_in.shape[0]
    n_attn = w_b_in.shape[0]
    caches = (cache_kv_w128, cache_kv_w512, cache_kv_w2048)
    dils = tuple(dl for _, dl in ATTN_PATTERNS)
    keeps = tuple(min(w, s) for w, _ in ATTN_PATTERNS)
    for c, dl in zip(caches, dils):
        assert c.shape[1] == SUB_BLOCK * dl
    tm = 256

    wb = lambda a: a.astype(BF16)
    row = lambda a: a.reshape(1, -1)
    w_a_in_b, w_a_group_b, w_a_out_b = wb(w_a_in), wb(w_a_group), wb(w_a_out)
    w_kv_b, w_b_in_b, w_b_out_b = wb(w_kv), wb(w_b_in), wb(w_b_out)
    q_cols = N_GROUPS * D_MODEL

    tab_p = _rope_tables(jnp.arange(s, dtype=jnp.int32))
    tab_s = _rope_tables(jnp.broadcast_to(PAST_LEN + jnp.arange(t_s, dtype=jnp.int32), (n_s,)))
    head_of_lane = jnp.arange(D_MODEL, dtype=jnp.int32) // HEAD_DIM
    seg = (head_of_lane[:, None] == jnp.arange(LANES, dtype=jnp.int32)[None, :]).astype(BF16)
    segt = seg.T

    h_p = x_prompt
    h_s = x_sample.reshape(n_s, d)
    state_t = jnp.transpose(state_pool, (0, 2, 1, 3))
    pool_p, pool_s = [], []
    for li in range(n_pool):
        h_p, ulast = _pool_layer(h_p, row(g_a[li]), w_a_in_b[li], w_a_group_b[li], row(a_scale[li]), w_a_out_b[li], tm=tm)
        pool_p.append(ulast[:, 16 - POOL_STATE:])
        h_s, nst = _spool_layer(h_s, state_t[li], row(g_a[li]), w_a_in_b[li],
                                w_a_group_b[li], row(a_scale[li]), w_a_out_b[li])
        pool_s.append(jnp.transpose(nst, (1, 0, 2)))

    kv_chunks = []
    for gi in range(N_GROUPS):
        kv_chunks.append((True, 1.0, dils[gi], True, (gi, 0)))
        kv_chunks.append((False, 1.0, dils[gi], True, (gi, D_MODEL)))
    kv_out = _proj(h_p, row(g_kv), w_kv_b, tab_p, tuple(kv_chunks), tuple((kp, 2 * D_MODEL) for kp in keeps), tm=tm)
    ks = [kv_out[2 * gi] for gi in range(N_GROUPS)]
    vs = [kv_out[2 * gi + 1] for gi in range(N_GROUPS)]
    kv_keep_p = kv_out[2 * N_GROUPS:]

    skv_chunks = []
    for gi in range(N_GROUPS):
        skv_chunks.append((True, 1.0, 1, False, (gi, 0)))
        skv_chunks.append((False, 1.0, 1, False, (gi, D_MODEL)))
    kvn = _proj(h_s[None], row(g_kv), w_kv_b, tab_s, tuple(skv_chunks), ((n_s, 2 * D_MODEL),) * N_GROUPS, tm=n_s)
    kvn = [a[0] for a in kvn]

    q_scale = 1.0 / math.sqrt(HEAD_DIM)
    q_chunks = tuple((True, q_scale, dils[gi], True, None) for gi in range(N_GROUPS))
    sq_chunks = tuple((True, q_scale, 1, False, (0, gi * D_MODEL)) for gi in range(N_GROUPS))
    cache_rows = [_gather_cache(jnp.transpose(c, (0, 2, 3, 4, 1)), dl) for c, dl in zip(caches, dils)]

    for lj in range(n_attn):
        final = lj == n_attn - 1
        w_q = w_b_in_b[lj][:, :q_cols]
        w_gate = w_b_in_b[lj][:, q_cols:]
        qs = _proj(h_p, row(g_b[lj]), w_q, tab_p, q_chunks, (), tm=tm)
        comb = _attention(qs, ks, vs)
        h_p = _outproj(h_p, comb, row(g_b[lj]), w_gate, w_b_out_b[lj], row(g_final), final=final, tm=tm)

        (q_s,) = _proj(h_s[None], row(g_b[lj]), w_q, tab_s, sq_chunks, ((n_s, q_cols),), tm=n_s)
        q_s = q_s[0]
        parts = [_sample_attention(q_s[:, gi * D_MODEL:(gi + 1) * D_MODEL], cache_rows[gi], seg, segt, bb=8)
                 for gi in range(N_GROUPS)]
        acc = jnp.stack([pt[0] for pt in parts])
        m = jnp.stack([pt[1] for pt in parts])
        l = jnp.stack([pt[2] for pt in parts])
        h_s = _sample_outproj(h_s, q_s, kvn, acc, m, l, seg, segt, row(g_b[lj]), w_gate, w_b_out_b[lj],
                              row(g_final), final=final)

    kv_shape = (2, N_HEADS, HEAD_DIM)
    outs = [h_p, h_s.reshape(n_s, t_s, d), jnp.stack(pool_p), jnp.stack(pool_s)]
    for gi in range(N_GROUPS):
        outs.append(kv_keep_p[gi].reshape(b, keeps[gi], *kv_shape))
        outs.append(kvn[gi].reshape(n_s, t_s, *kv_shape))
    return tuple(outs)
```

```python
import functools
import math
from typing import NamedTuple

import jax
import jax.numpy as jnp
from jax import lax
from jax.experimental import pallas as pl
from jax.experimental.pallas import tpu as pltpu

D_MODEL = 1024
HEAD_DIM = 64
N_HEADS = D_MODEL // HEAD_DIM
ROT_DIM = HEAD_DIM // 4
ROPE_THETA = 500000.0
POOL_WINDOWS = (2, 4, 8, 16)
POOL_GROUP_WIDTH = D_MODEL // len(POOL_WINDOWS)
POOL_STATE = max(POOL_WINDOWS) - 1
ATTN_PATTERNS = ((128, 1), (512, 4), (2048, 16))
N_GROUPS = len(ATTN_PATTERNS)
SUB_BLOCK = 128
RMS_EPS = 1e-6
PAST_LEN = 2048

LANES = 128
ATTN_TILE = SUB_BLOCK * max(d for _, d in ATTN_PATTERNS)
POOL_HIST = 32
VMEM_LIMIT = 56 * 1024 * 1024

F32 = jnp.float32
BF16 = jnp.bfloat16


def _rmsnorm(x, g):
    ms = jnp.mean(x * x, axis=-1, keepdims=True)
    return x * lax.rsqrt(ms + RMS_EPS) * g


def _silu(x):
    return x * (1.0 / (1.0 + jnp.exp(-x)))


def _const_spec(shape):
    return pl.BlockSpec(shape, lambda *_: (0,) * len(shape), pipeline_mode=pl.Buffered(1))


def _params(sem):
    return pltpu.CompilerParams(dimension_semantics=sem, vmem_limit_bytes=VMEM_LIMIT)


GATHER_FEATURES = 256


class _Gather(NamedTuple):
    cache_t: jax.Array
    sel: jax.Array
    rows_per_step: int
    first_step: int
    out: jax.Array | None


def _gather_body(x_ref, sel_ref, o_ref):
    nb, _, _, hd, n_pos = x_ref.shape
    heads = GATHER_FEATURES // hd
    sel = sel_ref[...]
    for bi in range(nb):
        for c in range(N_HEADS // heads):
            x = x_ref[bi, 0, heads * c:heads * (c + 1)].reshape(GATHER_FEATURES, n_pos).astype(BF16)
            y = lax.dot_general(sel, x, (((1,), (1,)), ((), ())), preferred_element_type=F32)
            o_ref[bi, :, c * GATHER_FEATURES:(c + 1) * GATHER_FEATURES] = y.astype(BF16)


def _tile_call(body, n_in, n_out, args, gather, *, grid, in_specs, out_specs, out_shape, scratch_shapes, name):
    if gather is None:
        outs = pl.pallas_call(body, grid=grid, in_specs=in_specs, out_specs=out_specs, out_shape=out_shape,
                              scratch_shapes=scratch_shapes, compiler_params=_params(("arbitrary",) * len(grid)),
                              name=name)(*args)
        return list(outs), None
    n_i = grid[1]
    nb = gather.rows_per_step
    n_kv, n_h, hd, n_pos = gather.cache_t.shape[1:]
    kept = gather.sel.shape[0]

    def slab(bi, i):
        step = gather.first_step + bi * n_i + i
        return step // n_kv, step % n_kv

    side_in = [gather.cache_t, gather.sel] + ([] if gather.out is None else [gather.out])
    n_side = len(side_in)

    def kern(*refs):
        x_ref, sel_ref = refs[n_in:n_in + 2]
        o_ref = refs[n_in + n_side + n_out]
        _gather_body(x_ref, sel_ref, o_ref)
        body(*refs[:n_in], *refs[n_in + n_side:n_in + n_side + n_out], *refs[n_in + n_side + n_out + 1:])

    side_specs = [pl.BlockSpec((nb, 1, n_h, hd, n_pos), lambda bi, i: (*slab(bi, i), 0, 0, 0)),
                  _const_spec(gather.sel.shape)]
    if gather.out is not None:
        side_specs.append(pl.BlockSpec(memory_space=pl.ANY))
    outs = pl.pallas_call(
        kern,
        grid=grid,
        in_specs=list(in_specs) + side_specs,
        out_specs=list(out_specs) + [pl.BlockSpec((nb, kept, n_h * hd), lambda bi, i: (slab(bi, i)[0], 0, slab(bi, i)[1]))],
        out_shape=list(out_shape) + [jax.ShapeDtypeStruct((gather.cache_t.shape[0], kept, n_kv * n_h * hd), BF16)],
        scratch_shapes=scratch_shapes,
        input_output_aliases={} if gather.out is None else {n_in + 2: n_out},
        compiler_params=_params(("arbitrary",) * len(grid)),
        name=name,
    )(*args, *side_in)
    return list(outs[:n_out]), outs[n_out]


def _pool_kernel(h_ref, g_ref, win_ref, wgrp_ref, scale_ref, wout_ref, out_ref, ulast_ref,
                 ext_ref, lvl_ref, *, tm):
    i = pl.program_id(1)
    n_i = pl.num_programs(1)
    hist = POOL_HIST

    @pl.when(i == 0)
    def _():
        ext_ref[0:hist, :] = jnp.zeros((hist, D_MODEL), F32)

    x = h_ref[0]
    hn = _rmsnorm(x, g_ref[...]).astype(BF16)
    proj = jnp.dot(hn, win_ref[...], preferred_element_type=F32)
    gate = proj[:, D_MODEL:]
    ext_ref[hist:hist + tm, :] = proj[:, :D_MODEL]

    total = hist + tm
    pos = i * tm + lax.broadcasted_iota(jnp.int32, (tm, 1), 0)
    src = ext_ref
    z_parts = []
    for k, w in enumerate(POOL_WINDOWS):
        shift = w // 2
        lo = 8 * (k + 1)
        c0 = k * POOL_GROUP_WIDTH
        cur = src[lo:total, c0:] + src[lo - shift:total - shift, c0:]
        dst = lvl_ref.at[k % 2]
        dst[lo:total, c0:] = cur
        inv_cnt = 1.0 / jnp.minimum(pos + 1, w).astype(F32)
        wsum = dst[hist:total, c0:c0 + POOL_GROUP_WIDTH]
        r = wsum * inv_cnt - ext_ref[hist:total, c0:c0 + POOL_GROUP_WIDTH]
        z_parts.append(jnp.dot(r.astype(BF16), wgrp_ref[k], preferred_element_type=F32))
        src = dst
    z = jnp.concatenate(z_parts, axis=-1)
    y = z * scale_ref[...] * _silu(gate)
    out_ref[0] = x + jnp.dot(y.astype(BF16), wout_ref[...], preferred_element_type=F32)

    @pl.when(i == n_i - 1)
    def _():
        ulast_ref[0] = ext_ref[hist + tm - 16:hist + tm, :]

    ext_ref[0:hist, :] = ext_ref[tm:tm + hist, :]


def _pool_layer(h, g, w_in, w_grp, scale, w_out, *, tm, gather=None):
    b, s, d = h.shape
    kern = functools.partial(_pool_kernel, tm=tm)
    return _tile_call(
        kern, 6, 2, (h, g, w_in, w_grp, scale, w_out), gather,
        grid=(b, s // tm),
        in_specs=[
            pl.BlockSpec((1, tm, d), lambda bi, i: (bi, i, 0)),
            _const_spec((1, d)),
            _const_spec((d, 2 * d)),
            _const_spec((len(POOL_WINDOWS), POOL_GROUP_WIDTH, POOL_GROUP_WIDTH)),
            _const_spec((1, d)),
            _const_spec((d, d)),
        ],
        out_specs=[
            pl.BlockSpec((1, tm, d), lambda bi, i: (bi, i, 0)),
            pl.BlockSpec((1, 16, d), lambda bi, i: (bi, 0, 0)),
        ],
        out_shape=[jax.ShapeDtypeStruct((b, s, d), F32), jax.ShapeDtypeStruct((b, 16, d), F32)],
        scratch_shapes=[pltpu.VMEM((POOL_HIST + tm, d), F32), pltpu.VMEM((2, POOL_HIST + tm, d), F32)],
        name="pool_layer",
    )


def _spool_kernel(h_ref, st_ref, g_ref, win_ref, wgrp_ref, scale_ref, wout_ref, out_ref, nst_ref):
    x = h_ref[...]
    hn = _rmsnorm(x, g_ref[...]).astype(BF16)
    proj = jnp.dot(hn, win_ref[...], preferred_element_type=F32)
    u = proj[:, :D_MODEL]
    gate = proj[:, D_MODEL:]
    z_parts = []
    for k, w in enumerate(POOL_WINDOWS):
        c0 = k * POOL_GROUP_WIDTH
        uk = u[:, c0:c0 + POOL_GROUP_WIDTH]
        tot = uk
        for j in range(1, w):
            tot = tot + st_ref[POOL_STATE - j, :, c0:c0 + POOL_GROUP_WIDTH]
        cnt = float(min(PAST_LEN + 1, w))
        r = tot / cnt - uk
        z_parts.append(jnp.dot(r.astype(BF16), wgrp_ref[k], preferred_element_type=F32))
    z = jnp.concatenate(z_parts, axis=-1)
    y = z * scale_ref[...] * _silu(gate)
    out_ref[...] = x + jnp.dot(y.astype(BF16), wout_ref[...], preferred_element_type=F32)
    nst_ref[0:POOL_STATE - 1] = st_ref[1:POOL_STATE]
    nst_ref[POOL_STATE - 1] = u


def _spool_layer(h, state_t, g, w_in, w_grp, scale, w_out):
    n, d = h.shape
    return pl.pallas_call(
        _spool_kernel,
        out_shape=[jax.ShapeDtypeStruct((n, d), F32), jax.ShapeDtypeStruct(state_t.shape, F32)],
        compiler_params=pltpu.CompilerParams(vmem_limit_bytes=VMEM_LIMIT),
        name="pool_layer_sample",
    )(h, state_t, g, w_in, w_grp, scale, w_out)


def _rope_tables(pos):
    half = ROT_DIM // 2
    inv = 1.0 / (ROPE_THETA ** (jnp.arange(0, ROT_DIM, 2, dtype=jnp.float32) / ROT_DIM))
    ang = pos.astype(jnp.float32)[:, None] * inv[None, :]
    cos, sin = jnp.cos(ang), jnp.sin(ang)
    t = pos.shape[0]
    rest = HEAD_DIM - ROT_DIM
    cos_h = jnp.concatenate([cos, cos, jnp.ones((t, rest), F32)], axis=1)
    sin_a = jnp.concatenate([jnp.zeros((t, half), F32), sin, jnp.zeros((t, rest), F32)], axis=1)
    sin_b = jnp.concatenate([-sin, jnp.zeros((t, half + rest), F32)], axis=1)
    reps = LANES // HEAD_DIM
    return tuple(jnp.tile(a, (1, reps)) for a in (cos_h, sin_a, sin_b))


def _proj_kernel(h_ref, g_ref, w_ref, cos_ref, sa_ref, sb_ref, *rest, tm, chunks, f32_outs, n_bf16):
    bf_refs = rest[:n_bf16]
    f32_refs = rest[n_bf16:n_bf16 + len(f32_outs)]
    y_ref = rest[-1]
    i = pl.program_id(1)
    n_i = pl.num_programs(1)
    half = ROT_DIM // 2

    hn = _rmsnorm(h_ref[0], g_ref[...]).astype(BF16)
    bi = 0
    n_lt = D_MODEL // LANES
    for c, (rope, scale, dil, bf_out, f32_slot) in enumerate(chunks):
        y = jnp.dot(hn, w_ref[:, c * D_MODEL:(c + 1) * D_MODEL], preferred_element_type=F32)
        for lt in range(n_lt):
            ys = y[:, lt * LANES:(lt + 1) * LANES]
            if rope:
                ys = (ys * cos_ref[...] + pltpu.roll(ys, half, 1) * sa_ref[...]
                      + pltpu.roll(ys, LANES - half, 1) * sb_ref[...])
            y_ref[lt] = ys * scale if scale != 1.0 else ys
        if f32_slot is not None:
            oi, off = f32_slot
            keep = f32_outs[oi][0]
            kb = min(tm, keep)
            first = n_i - keep // kb

            @pl.when(i >= first)
            def _(oi=oi, off=off, kb=kb):
                for lt in range(n_lt):
                    f32_refs[oi][0, :, off + lt * LANES:off + (lt + 1) * LANES] = y_ref[lt, tm - kb:tm, :]
        if bf_out:
            o_ref = bf_refs[bi]
            bi += 1
            for lt in range(n_lt):
                cols = slice(lt * LANES, (lt + 1) * LANES)
                if dil == 1:
                    o_ref[0, 0, :, cols] = y_ref[lt].astype(BF16)
                else:
                    for r in range(dil):
                        o_ref[0, r, :, cols] = y_ref[lt, pl.ds(r, tm // dil, stride=dil), :].astype(BF16)


def _proj(h, g, w, tables, chunks, f32_outs, *, tm, gather=None):
    b, s, d = h.shape
    n_i = s // tm
    out_shapes, out_specs = [], []
    for rope, scale, dil, bf_out, f32_slot in chunks:
        if bf_out:
            out_shapes.append(jax.ShapeDtypeStruct((b, dil, s // dil, d), BF16))
            out_specs.append(pl.BlockSpec((1, dil, tm // dil, d), lambda bi, i: (bi, 0, i, 0)))
    n_bf16 = len(out_shapes)
    for keep, width in f32_outs:
        kb = min(tm, keep)
        first = n_i - keep // kb
        out_shapes.append(jax.ShapeDtypeStruct((b, keep, width), F32))
        out_specs.append(pl.BlockSpec((1, kb, width), lambda bi, i, first=first: (bi, jnp.maximum(i - first, 0), 0)))
    kern = functools.partial(_proj_kernel, tm=tm, chunks=chunks, f32_outs=f32_outs, n_bf16=n_bf16)
    tab_spec = pl.BlockSpec((tm, LANES), lambda bi, i: (i, 0))
    return _tile_call(
        kern, 6, len(out_shapes), (h, g, w, *tables), gather,
        grid=(b, n_i),
        in_specs=[
            pl.BlockSpec((1, tm, d), lambda bi, i: (bi, i, 0)),
            _const_spec((1, d)),
            _const_spec(w.shape),
            tab_spec, tab_spec, tab_spec,
        ],
        out_specs=out_specs,
        out_shape=out_shapes,
        scratch_shapes=[pltpu.VMEM((d // LANES, tm, LANES), F32)],
        name="proj_rope",
    )


def _attn_kernel(*refs):
    q_refs = refs[0:3]
    k_refs = refs[3:6]
    v_refs = refs[6:9]
    kp_refs = refs[9:12]
    vp_refs = refs[12:15]
    out_ref = refs[15]
    acc_ref, m_ref, l_ref = refs[16:19]
    i = pl.program_id(1)
    blk = SUB_BLOCK

    lane_q = lax.broadcasted_iota(jnp.int32, (blk, LANES), 1) < HEAD_DIM
    key = lax.broadcasted_iota(jnp.int32, (blk, 2 * blk), 1)
    qry = lax.broadcasted_iota(jnp.int32, (blk, 2 * blk), 0)
    band = (key >= qry) & (key <= qry + blk)
    band_first = band & (key >= jnp.where(i > 0, 0, blk))
    ones_v = jnp.ones((2 * blk, LANES), BF16)

    def sub_block(g, dil, r, sub):
        rows_q = slice(sub * blk, (sub + 1) * blk)
        qb = q_refs[g][0, r, rows_q, :]
        if sub == 0:
            kcat = jnp.concatenate([kp_refs[g][0, r], k_refs[g][0, r, 0:blk, :]], axis=0)
            vcat = jnp.concatenate([vp_refs[g][0, r], v_refs[g][0, r, 0:blk, :]], axis=0)
            valid = band_first
        else:
            kcat = k_refs[g][0, r, (sub - 1) * blk:(sub + 1) * blk, :]
            vcat = v_refs[g][0, r, (sub - 1) * blk:(sub + 1) * blk, :]
            valid = band
        parts = []
        for first in (True, False):
            qm = jnp.where(lane_q, qb, jnp.zeros_like(qb)) if first else jnp.where(lane_q, jnp.zeros_like(qb), qb)
            s = lax.dot_general(qm, kcat, (((1,), (1,)), ((), ())), preferred_element_type=F32)
            s = jnp.where(valid, s, -jnp.inf)
            m = jnp.max(s, axis=-1, keepdims=True)
            p = jnp.exp(s - m).astype(BF16)
            pv = jnp.dot(p, jnp.concatenate([vcat, ones_v], axis=1), preferred_element_type=F32)
            parts.append((pv, m))
        (pv0, m0), (pv1, m1) = parts
        rows = pl.ds(pl.multiple_of(r * (ATTN_TILE // dil) + sub * blk, blk), blk)
        acc_ref[g, rows, :] = jnp.where(lane_q, pv0[:, :LANES], pv1[:, :LANES])
        l_ref[g, rows, :] = jnp.where(lane_q, pv0[:, LANES:], pv1[:, LANES:])
        m_ref[g, rows, :] = jnp.where(lane_q, m0, m1)

    for g, (_, dil) in enumerate(ATTN_PATTERNS):
        nsub = ATTN_TILE // (dil * blk)

        def residue(r, carry, g=g, dil=dil, nsub=nsub):
            for sub in range(nsub):
                sub_block(g, dil, r, sub)
            return carry

        if dil == 1:
            residue(0, 0)
        else:
            lax.fori_loop(0, dil, residue, 0, unroll=min(dil, max(1, 8 // nsub)))

    big = ATTN_TILE // blk

    def merge(rb, carry):
        num = jnp.zeros((blk, LANES), F32)
        den = jnp.zeros((blk, LANES), F32)
        rows_g = []
        for g, (_, dil) in enumerate(ATTN_PATTERNS):
            step = big // dil
            start = (rb % dil) * (ATTN_TILE // dil) + rb // dil
            rows_g.append(pl.ds(start, blk, stride=step) if step > 1 else pl.ds(pl.multiple_of(start, blk), blk))
        ms = [m_ref[g, rows_g[g], :] for g in range(N_GROUPS)]
        mx = jnp.maximum(jnp.maximum(ms[0], ms[1]), ms[2])
        for g in range(N_GROUPS):
            e = jnp.exp(ms[g] - mx)
            num = num + e * acc_ref[g, rows_g[g], :]
            den = den + e * l_ref[g, rows_g[g], :]
        out_ref[0, pl.ds(rb, blk, stride=big), :] = num / den
        return carry

    lax.fori_loop(0, big, merge, 0, unroll=4)


def _attention(qs, ks, vs):
    b = qs[0].shape[0]
    s = qs[0].shape[1] * qs[0].shape[2]
    n_i = s // ATTN_TILE
    n_p = D_MODEL // LANES
    blk = SUB_BLOCK
    cur_specs, prev_specs = [], []
    for _, dil in ATTN_PATTERNS:
        per = ATTN_TILE // dil
        cur_specs.append(pl.BlockSpec((1, dil, per, LANES), lambda bi, i, p: (bi, 0, i, p)))
        nsub = per // blk
        prev_specs.append(pl.BlockSpec((1, dil, blk, LANES),
                                       lambda bi, i, p, nsub=nsub: (bi, 0, jnp.maximum(i * nsub - 1, 0), p)))
    return pl.pallas_call(
        _attn_kernel,
        grid=(b, n_i, n_p),
        in_specs=cur_specs * 3 + prev_specs * 2,
        out_specs=pl.BlockSpec((1, ATTN_TILE, LANES), lambda bi, i, p: (bi, i, p)),
        out_shape=jax.ShapeDtypeStruct((b, s, D_MODEL), F32),
        scratch_shapes=[pltpu.VMEM((N_GROUPS, ATTN_TILE, LANES), F32)] * 3,
        compiler_params=_params(("arbitrary", "arbitrary", "arbitrary")),
        name="dilated_attention",
    )(*qs, *ks, *vs, *ks, *vs)


def _outproj_kernel(h_ref, comb_ref, g_ref, wg_ref, wo_ref, gf_ref, out_ref, *, final):
    x = h_ref[0]
    hn = _rmsnorm(x, g_ref[...]).astype(BF16)
    gate = jnp.dot(hn, wg_ref[...], preferred_element_type=F32)
    y = comb_ref[0] * _silu(gate)
    o = x + jnp.dot(y.astype(BF16), wo_ref[...], preferred_element_type=F32)
    if final:
        o = _rmsnorm(o, gf_ref[...])
    out_ref[0] = o


def _outproj(h, comb, g, w_gate, w_out, g_final, *, final, tm, gather=None):
    b, s, d = h.shape
    kern = functools.partial(_outproj_kernel, final=final)
    tile = pl.BlockSpec((1, tm, d), lambda bi, i: (bi, i, 0))
    return _tile_call(
        kern, 6, 1, (h, comb, g, w_gate, w_out, g_final), gather,
        grid=(b, s // tm),
        in_specs=[tile, tile, _const_spec((1, d)), _const_spec((d, d)), _const_spec((d, d)), _const_spec((1, d))],
        out_specs=[tile],
        out_shape=[jax.ShapeDtypeStruct((b, s, d), F32)],
        scratch_shapes=[],
        name="gate_outproj",
    )


def _sattn_kernel(q_ref, c_ref, seg_ref, segt_ref, acc_ref, m_ref, l_ref, *, bb):
    n = c_ref.shape[1]
    q = q_ref[...]
    k = c_ref[:, :, 0:D_MODEL].astype(F32)
    v = c_ref[:, :, D_MODEL:].astype(F32)
    prod = (k * q[:, None, :]).reshape(bb * n, D_MODEL).astype(BF16)
    s = jnp.dot(prod, seg_ref[...], preferred_element_type=F32).reshape(bb, n, LANES)
    m = jnp.max(s, axis=1)
    p = jnp.exp(s - m[:, None, :])
    l = jnp.sum(p, axis=1)
    pe = jnp.dot(p.reshape(bb * n, LANES).astype(BF16), segt_ref[...], preferred_element_type=F32)
    acc_ref[...] = jnp.sum(pe.reshape(bb, n, D_MODEL) * v, axis=1)
    m_ref[...] = m
    l_ref[...] = l


def _sample_attention(q, rows, seg, segt, *, bb):
    n_b, n_rows, width = rows.shape
    kern = functools.partial(_sattn_kernel, bb=bb)
    return pl.pallas_call(
        kern,
        grid=(n_b // bb,),
        in_specs=[
            pl.BlockSpec((bb, D_MODEL), lambda i: (i, 0)),
            pl.BlockSpec((bb, n_rows, width), lambda i: (i, 0, 0)),
            _const_spec(seg.shape),
            _const_spec(segt.shape),
        ],
        out_specs=[
            pl.BlockSpec((bb, D_MODEL), lambda i: (i, 0)),
            pl.BlockSpec((bb, LANES), lambda i: (i, 0)),
            pl.BlockSpec((bb, LANES), lambda i: (i, 0)),
        ],
        out_shape=[
            jax.ShapeDtypeStruct((n_b, D_MODEL), F32),
            jax.ShapeDtypeStruct((n_b, LANES), F32),
            jax.ShapeDtypeStruct((n_b, LANES), F32),
        ],
        compiler_params=_params(("arbitrary",)),
        name="sample_attention",
    )(q, rows, seg, segt)


def _expand_heads(x, segt):
    x1 = x.astype(BF16)
    r1 = x - x1.astype(F32)
    x2 = r1.astype(BF16)
    x3 = (r1 - x2.astype(F32)).astype(BF16)
    out = jnp.dot(x1, segt, preferred_element_type=F32)
    out = out + jnp.dot(x2, segt, preferred_element_type=F32)
    return out + jnp.dot(x3, segt, preferred_element_type=F32)


def _soutproj_kernel(h_ref, q_ref, kvn0_ref, kvn1_ref, kvn2_ref, acc_ref, m_ref, l_ref, seg_ref, segt_ref,
                     g_ref, wg_ref, wo_ref, gf_ref, out_ref, *, final):
    kvn_refs = (kvn0_ref, kvn1_ref, kvn2_ref)
    seg, segt = seg_ref[...], segt_ref[...]
    s_new = []
    for g in range(N_GROUPS):
        qg = q_ref[:, g * D_MODEL:(g + 1) * D_MODEL]
        kn = kvn_refs[g][:, 0:D_MODEL]
        s_new.append(jnp.dot((qg * kn).astype(BF16), seg, preferred_element_type=F32))
    ms = [m_ref[g] for g in range(N_GROUPS)]
    mx = ms[0]
    for t in ms[1:] + s_new:
        mx = jnp.maximum(mx, t)
    e_old = [jnp.exp(ms[g] - mx) for g in range(N_GROUPS)]
    e_new = [jnp.exp(s_new[g] - mx) for g in range(N_GROUPS)]
    den = jnp.zeros_like(mx)
    for g in range(N_GROUPS):
        den = den + e_old[g] * l_ref[g] + e_new[g]
    comb = jnp.zeros(h_ref.shape, F32)
    for g in range(N_GROUPS):
        comb = comb + _expand_heads(e_old[g] / den, segt) * acc_ref[g]
        comb = comb + _expand_heads(e_new[g] / den, segt) * kvn_refs[g][:, D_MODEL:]

    x = h_ref[...]
    hn = _rmsnorm(x, g_ref[...]).astype(BF16)
    gate = jnp.dot(hn, wg_ref[...], preferred_element_type=F32)
    y = comb * _silu(gate)
    o = x + jnp.dot(y.astype(BF16), wo_ref[...], preferred_element_type=F32)
    if final:
        o = _rmsnorm(o, gf_ref[...])
    out_ref[...] = o


def _sample_outproj(h, q, kvn, acc, m, l, seg, segt, g, w_gate, w_out, g_final, *, final):
    kern = functools.partial(_soutproj_kernel, final=final)
    return pl.pallas_call(
        kern,
        out_shape=jax.ShapeDtypeStruct(h.shape, F32),
        compiler_params=pltpu.CompilerParams(vmem_limit_bytes=VMEM_LIMIT),
        name="gate_outproj_sample",
    )(h, q, *kvn, acc, m, l, seg, segt, g, w_gate, w_out, g_final)


def kernel(x_prompt, x_sample, state_pool, cache_kv_w128, cache_kv_w512, cache_kv_w2048, g_a, w_a_in, w_a_group,
           a_scale, w_a_out, g_kv, w_kv, g_b, w_b_in, w_b_out, g_final):
    b, s, d = x_prompt.shape
    n_s, t_s, _ = x_sample.shape
    assert d == D_MODEL and t_s == 1 and s % ATTN_TILE == 0
    n_pool = w_a_in.shape[0]
    n_attn = w_b_in.shape[0]
    caches = (cache_kv_w128, cache_kv_w512, cache_kv_w2048)
    dils = tuple(dl for _, dl in ATTN_PATTERNS)
    keeps = tuple(min(w, s) for w, _ in ATTN_PATTERNS)
    for c, dl in zip(caches, dils):
        assert c.shape[1] == SUB_BLOCK * dl
    tm = 256
    steps = b * (s // tm)

    slab_positions = 2048
    gathered, sels, cache_ts, rows_per_step, pending = [], [], [], [], []
    for ci, (c, dl) in enumerate(zip(caches, dils)):
        n_pos = c.shape[1]
        n_kv = c.shape[2]
        kept = n_pos // dl
        nb = max(1, min(slab_positions // n_pos, n_s * n_kv // steps))
        assert (n_s * n_kv) % (nb * steps) == 0
        cache_ts.append(jnp.transpose(c, (0, 2, 3, 4, 1)))
        sels.append((jnp.arange(n_pos, dtype=jnp.int32)[None, :]
                     == dl * jnp.arange(kept, dtype=jnp.int32)[:, None]).astype(BF16))
        rows_per_step.append(nb)
        n_calls = n_s * n_kv // (nb * steps)
        gathered.append(None if n_calls == 1 else jnp.zeros((n_s, kept, n_kv * D_MODEL), BF16))
        pending += [(ci, first) for first in range(0, n_calls * steps, steps)]
    pending.sort(key=lambda job: -caches[job[0]].shape[1])

    def hosted(fn, *args, **kwargs):
        if not pending:
            return fn(*args, **kwargs)[0]
        ci, first = pending.pop(0)
        job = _Gather(cache_ts[ci], sels[ci], rows_per_step[ci], first, gathered[ci])
        outs, gathered[ci] = fn(*args, gather=job, **kwargs)
        return outs

    wb = lambda a: a.astype(BF16)
    row = lambda a: a.reshape(1, -1)
    w_a_in_b, w_a_group_b, w_a_out_b = wb(w_a_in), wb(w_a_group), wb(w_a_out)
    w_kv_b, w_b_in_b, w_b_out_b = wb(w_kv), wb(w_b_in), wb(w_b_out)
    q_cols = N_GROUPS * D_MODEL

    tab_p = _rope_tables(jnp.arange(s, dtype=jnp.int32))
    tab_s = _rope_tables(jnp.broadcast_to(PAST_LEN + jnp.arange(t_s, dtype=jnp.int32), (n_s,)))
    head_of_lane = jnp.arange(D_MODEL, dtype=jnp.int32) // HEAD_DIM
    seg = (head_of_lane[:, None] == jnp.arange(LANES, dtype=jnp.int32)[None, :]).astype(BF16)
    segt = seg.T

    h_p = x_prompt
    h_s = x_sample.reshape(n_s, d)
    state_t = jnp.transpose(state_pool, (0, 2, 1, 3))
    pool_p, pool_s = [], []
    for li in range(n_pool):
        h_p, ulast = hosted(_pool_layer, h_p, row(g_a[li]), w_a_in_b[li], w_a_group_b[li], row(a_scale[li]),
                            w_a_out_b[li], tm=tm)
        pool_p.append(ulast[:, 16 - POOL_STATE:])
        h_s, nst = _spool_layer(h_s, state_t[li], row(g_a[li]), w_a_in_b[li],
                                w_a_group_b[li], row(a_scale[li]), w_a_out_b[li])
        pool_s.append(jnp.transpose(nst, (1, 0, 2)))

    kv_chunks = []
    for gi in range(N_GROUPS):
        kv_chunks.append((True, 1.0, dils[gi], True, (gi, 0)))
        kv_chunks.append((False, 1.0, dils[gi], True, (gi, D_MODEL)))
    kv_out, _ = _proj(h_p, row(g_kv), w_kv_b, tab_p, tuple(kv_chunks), tuple((kp, 2 * D_MODEL) for kp in keeps), tm=tm)
    ks =[kv_out[2 * gi] for gi in range(N_GROUPS)]
    vs = [kv_out[2 * gi + 1] for gi in range(N_GROUPS)]
    kv_keep_p = kv_out[2 * N_GROUPS:]

    skv_chunks = []
    for gi in range(N_GROUPS):
        skv_chunks.append((True, 1.0, 1, False, (gi, 0)))
        skv_chunks.append((False, 1.0, 1, False, (gi, D_MODEL)))
    kvn, _ = _proj(h_s[None], row(g_kv), w_kv_b, tab_s, tuple(skv_chunks), ((n_s, 2 * D_MODEL),) * N_GROUPS, tm=n_s)
    kvn = [a[0] for a in kvn]

    q_scale = 1.0 / math.sqrt(HEAD_DIM)
    q_chunks = tuple((True, q_scale, dils[gi], True, None) for gi in range(N_GROUPS))
    sq_chunks = tuple((True, q_scale, 1, False, (0, gi * D_MODEL)) for gi in range(N_GROUPS))
    w_qs = [w_b_in_b[lj][:, :q_cols] for lj in range(n_attn)]
    w_gates = [w_b_in_b[lj][:, q_cols:] for lj in range(n_attn)]

    for lj in range(n_attn):
        final = lj == n_attn - 1
        qs = hosted(_proj, h_p, row(g_b[lj]), w_qs[lj], tab_p, q_chunks, (), tm=tm)
        comb = _attention(qs, ks, vs)
        (h_p,) = hosted(_outproj, h_p, comb, row(g_b[lj]), w_gates[lj], w_b_out_b[lj], row(g_final), final=final, tm=tm)
    assert not pending

    for lj in range(n_attn):
        final = lj == n_attn - 1
        w_q, w_gate = w_qs[lj], w_gates[lj]
        (q_s,), _ = _proj(h_s[None], row(g_b[lj]), w_q, tab_s, sq_chunks, ((n_s, q_cols),), tm=n_s)
        q_s = q_s[0]
        parts = [_sample_attention(q_s[:, gi * D_MODEL:(gi + 1) * D_MODEL], gathered[gi], seg, segt, bb=8)
                 for gi in range(N_GROUPS)]
        acc = jnp.stack([pt[0] for pt in parts])
        m = jnp.stack([pt[1] for pt in parts])
        l = jnp.stack([pt[2] for pt in parts])
        h_s = _sample_outproj(h_s, q_s, kvn, acc, m, l, seg, segt, row(g_b[lj]), w_gate, w_b_out_b[lj],
                              row(g_final), final=final)

    kv_shape = (2, N_HEADS, HEAD_DIM)
    outs = [h_p, h_s.reshape(n_s, t_s, d), jnp.stack(pool_p), jnp.stack(pool_s)]
    for gi in range(N_GROUPS):
        outs.append(kv_keep_p[gi].reshape(b, keeps[gi], *kv_shape))
        outs.append(kvn[gi].reshape(n_s, t_s, *kv_shape))
    return tuple(outs)
```

```python
import functools
import math
from typing import NamedTuple

import jax
import jax.numpy as jnp
from jax import lax
from jax.experimental import pallas as pl
from jax.experimental.pallas import tpu as pltpu

D_MODEL = 1024
HEAD_DIM = 64
N_HEADS = D_MODEL // HEAD_DIM
ROT_DIM = HEAD_DIM // 4
ROPE_THETA = 500000.0
POOL_WINDOWS = (2, 4, 8, 16)
POOL_GROUP_WIDTH = D_MODEL // len(POOL_WINDOWS)
POOL_STATE = max(POOL_WINDOWS) - 1
ATTN_PATTERNS = ((128, 1), (512, 4), (2048, 16))
N_GROUPS = len(ATTN_PATTERNS)
SUB_BLOCK = 128
RMS_EPS = 1e-6
PAST_LEN = 2048

LANES = 128
ATTN_TILE = SUB_BLOCK * max(d for _, d in ATTN_PATTERNS)
POOL_HIST = 32
VMEM_LIMIT = 56 * 1024 * 1024

F32 = jnp.float32
BF16 = jnp.bfloat16


def _rmsnorm(x, g):
    ms = jnp.mean(x * x, axis=-1, keepdims=True)
    return x * lax.rsqrt(ms + RMS_EPS) * g


def _silu(x):
    return x * (1.0 / (1.0 + jnp.exp(-x)))


def _const_spec(shape):
    return pl.BlockSpec(shape, lambda *_: (0,) * len(shape), pipeline_mode=pl.Buffered(1))


def _params(sem):
    return pltpu.CompilerParams(dimension_semantics=sem, vmem_limit_bytes=VMEM_LIMIT)


GATHER_FEATURES = 256


class _Gather(NamedTuple):
    cache_t: jax.Array
    sel: jax.Array
    rows_per_step: int
    first_step: int
    out: jax.Array | None


def _gather_body(x_ref, sel_ref, o_ref):
    nb, _, _, hd, n_pos = x_ref.shape
    heads = GATHER_FEATURES // hd
    sel = sel_ref[...]
    for bi in range(nb):
        for c in range(N_HEADS // heads):
            x = x_ref[bi, 0, heads * c:heads * (c + 1)].reshape(GATHER_FEATURES, n_pos).astype(BF16)
            y = lax.dot_general(sel, x, (((1,), (1,)), ((), ())), preferred_element_type=F32)
            o_ref[bi, :, c * GATHER_FEATURES:(c + 1) * GATHER_FEATURES] = y.astype(BF16)


def _tile_call(body, n_in, n_out, args, gather, *, grid, in_specs, out_specs, out_shape, scratch_shapes, name):
    if gather is None:
        outs = pl.pallas_call(body, grid=grid, in_specs=in_specs, out_specs=out_specs, out_shape=out_shape,
                              scratch_shapes=scratch_shapes, compiler_params=_params(("arbitrary",) * len(grid)),
                              name=name)(*args)
        return list(outs), None
    n_i = grid[1]
    nb = gather.rows_per_step
    n_kv, n_h, hd, n_pos = gather.cache_t.shape[1:]
    kept = gather.sel.shape[0]

    def slab(bi, i):
        step = gather.first_step + bi * n_i + i
        return step // n_kv, step % n_kv

    side_in = [gather.cache_t, gather.sel] + ([] if gather.out is None else [gather.out])
    n_side = len(side_in)

    def kern(*refs):
        x_ref, sel_ref = refs[n_in:n_in + 2]
        o_ref = refs[n_in + n_side + n_out]
        _gather_body(x_ref, sel_ref, o_ref)
        body(*refs[:n_in], *refs[n_in + n_side:n_in + n_side + n_out], *refs[n_in + n_side + n_out + 1:])

    side_specs = [pl.BlockSpec((nb, 1, n_h, hd, n_pos), lambda bi, i: (*slab(bi, i), 0, 0, 0)),
                  _const_spec(gather.sel.shape)]
    if gather.out is not None:
        side_specs.append(pl.BlockSpec(memory_space=pl.ANY))
    outs = pl.pallas_call(
        kern,
        grid=grid,
        in_specs=list(in_specs) + side_specs,
        out_specs=list(out_specs) + [pl.BlockSpec((nb, kept, n_h * hd), lambda bi, i: (slab(bi, i)[0], 0, slab(bi, i)[1]))],
        out_shape=list(out_shape) + [jax.ShapeDtypeStruct((gather.cache_t.shape[0], kept, n_kv * n_h * hd), BF16)],
        scratch_shapes=scratch_shapes,
        input_output_aliases={} if gather.out is None else {n_in + 2: n_out},
        compiler_params=_params(("arbitrary",) * len(grid)),
        name=name,
    )(*args, *side_in)
    return list(outs[:n_out]), outs[n_out]


def _pool_kernel(h_ref, g_ref, win_ref, wgrp_ref, scale_ref, wout_ref, out_ref, ulast_ref,
                 ext_ref, lvl_ref, *, tm):
    i = pl.program_id(1)
    hist = POOL_HIST

    @pl.when(i == 0)
    def _():
        ext_ref[0:hist, :] = jnp.zeros((hist, D_MODEL), F32)

    x = h_ref[0]
    hn = _rmsnorm(x, g_ref[...]).astype(BF16)
    proj = jnp.dot(hn, win_ref[...], preferred_element_type=F32)
    gate = proj[:, D_MODEL:]
    ext_ref[hist:hist + tm, :] = proj[:, :D_MODEL]

    total = hist + tm
    pos = i * tm + lax.broadcasted_iota(jnp.int32, (tm, 1), 0)
    src = ext_ref
    z_parts = []
    for k, w in enumerate(POOL_WINDOWS):
        shift = w // 2
        lo = 8 * (k + 1)
        c0 = k * POOL_GROUP_WIDTH
        cur = src[lo:total, c0:] + src[lo - shift:total - shift, c0:]
        dst = lvl_ref.at[k % 2]
        dst[lo:total, c0:] = cur
        inv_cnt = 1.0 / jnp.minimum(pos + 1, w).astype(F32)
        wsum = dst[hist:total, c0:c0 + POOL_GROUP_WIDTH]
        r = wsum * inv_cnt - ext_ref[hist:total, c0:c0 + POOL_GROUP_WIDTH]
        z_parts.append(jnp.dot(r.astype(BF16), wgrp_ref[k], preferred_element_type=F32))
        src = dst
    z = jnp.concatenate(z_parts, axis=-1)
    y = z * scale_ref[...] * _silu(gate)
    out_ref[0] = x + jnp.dot(y.astype(BF16), wout_ref[...], preferred_element_type=F32)

    ulast_ref[0] = ext_ref[hist + tm - 16:hist + tm, :]
    ext_ref[0:hist, :] = ext_ref[tm:tm + hist, :]


def _pool_layer(h, g, w_in, w_grp, scale, w_out, *, tm, gather=None):
    b, s, d = h.shape
    kern = functools.partial(_pool_kernel, tm=tm)
    return _tile_call(
        kern, 6, 2, (h, g, w_in, w_grp, scale, w_out), gather,
        grid=(b, s // tm),
        in_specs=[
            pl.BlockSpec((1, tm, d), lambda bi, i: (bi, i, 0)),
            _const_spec((1, d)),
            _const_spec((d, 2 * d)),
            _const_spec((len(POOL_WINDOWS), POOL_GROUP_WIDTH, POOL_GROUP_WIDTH)),
            _const_spec((1, d)),
            _const_spec((d, d)),
        ],
        out_specs=[
            pl.BlockSpec((1, tm, d), lambda bi, i: (bi, i, 0)),
            pl.BlockSpec((1, 16, d), lambda bi, i: (bi, 0, 0)),
        ],
        out_shape=[jax.ShapeDtypeStruct((b, s, d), F32), jax.ShapeDtypeStruct((b, 16, d), F32)],
        scratch_shapes=[pltpu.VMEM((POOL_HIST + tm, d), F32), pltpu.VMEM((2, POOL_HIST + tm, d), F32)],
        name="pool_layer",
    )


def _spool_kernel(h_ref, st_ref, g_ref, win_ref, wgrp_ref, scale_ref, wout_ref, out_ref, nst_ref):
    x = h_ref[...]
    hn = _rmsnorm(x, g_ref[...]).astype(BF16)
    proj = jnp.dot(hn, win_ref[...], preferred_element_type=F32)
    u = proj[:, :D_MODEL]
    gate = proj[:, D_MODEL:]
    z_parts = []
    for k, w in enumerate(POOL_WINDOWS):
        c0 = k * POOL_GROUP_WIDTH
        uk = u[:, c0:c0 + POOL_GROUP_WIDTH]
        tot = uk
        for j in range(1, w):
            tot = tot + st_ref[POOL_STATE - j, :, c0:c0 + POOL_GROUP_WIDTH]
        cnt = float(min(PAST_LEN + 1, w))
        r = tot / cnt - uk
        z_parts.append(jnp.dot(r.astype(BF16), wgrp_ref[k], preferred_element_type=F32))
    z = jnp.concatenate(z_parts, axis=-1)
    y = z * scale_ref[...] * _silu(gate)
    out_ref[...] = x + jnp.dot(y.astype(BF16), wout_ref[...], preferred_element_type=F32)
    nst_ref[0:POOL_STATE - 1] = st_ref[1:POOL_STATE]
    nst_ref[POOL_STATE - 1] = u


def _spool_layer(h, state_t, g, w_in, w_grp, scale, w_out):
    n, d = h.shape
    return pl.pallas_call(
        _spool_kernel,
        out_shape=[jax.ShapeDtypeStruct((n, d), F32), jax.ShapeDtypeStruct(state_t.shape, F32)],
        compiler_params=pltpu.CompilerParams(vmem_limit_bytes=VMEM_LIMIT),
        name="pool_layer_sample",
    )(h, state_t, g, w_in, w_grp, scale, w_out)


def _rope_tables(pos):
    half = ROT_DIM // 2
    inv = 1.0 / (ROPE_THETA ** (jnp.arange(0, ROT_DIM, 2, dtype=jnp.float32) / ROT_DIM))
    ang = pos.astype(jnp.float32)[:, None] * inv[None, :]
    cos, sin = jnp.cos(ang), jnp.sin(ang)
    t = pos.shape[0]
    rest = HEAD_DIM - ROT_DIM
    cos_h = jnp.concatenate([cos, cos, jnp.ones((t, rest), F32)], axis=1)
    sin_a = jnp.concatenate([jnp.zeros((t, half), F32), sin, jnp.zeros((t, rest), F32)], axis=1)
    sin_b = jnp.concatenate([-sin, jnp.zeros((t, half + rest), F32)], axis=1)
    reps = LANES // HEAD_DIM
    return tuple(jnp.tile(a, (1, reps)) for a in (cos_h, sin_a, sin_b))


def _proj_kernel(h_ref, g_ref, w_ref, cos_ref, sa_ref, sb_ref, *rest, tm, chunks, f32_outs, n_bf16):
    bf_refs = rest[:n_bf16]
    f32_refs = rest[n_bf16:n_bf16 + len(f32_outs)]
    y_ref = rest[-1]
    half = ROT_DIM // 2

    hn = _rmsnorm(h_ref[0], g_ref[...]).astype(BF16)
    bi = 0
    n_lt = D_MODEL // LANES
    for c, (rope, scale, dil, bf_out, f32_slot) in enumerate(chunks):
        y = jnp.dot(hn, w_ref[:, c * D_MODEL:(c + 1) * D_MODEL], preferred_element_type=F32)
        for lt in range(n_lt):
            ys = y[:, lt * LANES:(lt + 1) * LANES]
            if rope:
                ys = (ys * cos_ref[...] + pltpu.roll(ys, half, 1) * sa_ref[...]
                      + pltpu.roll(ys, LANES - half, 1) * sb_ref[...])
            y_ref[lt] = ys * scale if scale != 1.0 else ys
        if f32_slot is not None:
            oi, off = f32_slot
            kb = min(tm, f32_outs[oi][0])
            for lt in range(n_lt):
                f32_refs[oi][0, :, off + lt * LANES:off + (lt + 1) * LANES] = y_ref[lt, tm - kb:tm, :]
        if bf_out:
            o_ref = bf_refs[bi]
            bi += 1
            for lt in range(n_lt):
                cols = slice(lt * LANES, (lt + 1) * LANES)
                if dil == 1:
                    o_ref[0, 0, :, cols] = y_ref[lt].astype(BF16)
                else:
                    for r in range(dil):
                        o_ref[0, r, :, cols] = y_ref[lt, pl.ds(r, tm // dil, stride=dil), :].astype(BF16)


def _proj(h, g, w, tables, chunks, f32_outs, *, tm, gather=None):
    b, s, d = h.shape
    n_i = s // tm
    out_shapes, out_specs = [], []
    for rope, scale, dil, bf_out, f32_slot in chunks:
        if bf_out:
            out_shapes.append(jax.ShapeDtypeStruct((b, dil, s // dil, d), BF16))
            out_specs.append(pl.BlockSpec((1, dil, tm // dil, d), lambda bi, i: (bi, 0, i, 0)))
    n_bf16 = len(out_shapes)
    for keep, width in f32_outs:
        kb = min(tm, keep)
        first = n_i - keep // kb
        out_shapes.append(jax.ShapeDtypeStruct((b, keep, width), F32))
        out_specs.append(pl.BlockSpec((1, kb, width), lambda bi, i, first=first: (bi, jnp.maximum(i - first, 0), 0)))
    kern = functools.partial(_proj_kernel, tm=tm, chunks=chunks, f32_outs=f32_outs, n_bf16=n_bf16)
    tab_spec = pl.BlockSpec((tm, LANES), lambda bi, i: (i, 0))
    return _tile_call(
        kern, 6, len(out_shapes), (h, g, w, *tables), gather,
        grid=(b, n_i),
        in_specs=[
            pl.BlockSpec((1, tm, d), lambda bi, i: (bi, i, 0)),
            _const_spec((1, d)),
            _const_spec(w.shape),
            tab_spec, tab_spec, tab_spec,
        ],
        out_specs=out_specs,
        out_shape=out_shapes,
        scratch_shapes=[pltpu.VMEM((d // LANES, tm, LANES), F32)],
        name="proj_rope",
    )


def _attn_kernel(*refs):
    q_refs = refs[0:3]
    k_refs = refs[3:6]
    v_refs = refs[6:9]
    kp_refs = refs[9:12]
    vp_refs = refs[12:15]
    out_ref = refs[15]
    acc_ref, m_ref, l_ref = refs[16:19]
    i = pl.program_id(1)
    blk = SUB_BLOCK

    lane_q = lax.broadcasted_iota(jnp.int32, (blk, LANES), 1) < HEAD_DIM
    key = lax.broadcasted_iota(jnp.int32, (2 * blk, 2 * blk), 1)
    qry = lax.broadcasted_iota(jnp.int32, (2 * blk, 2 * blk), 0) % blk
    band = (key >= qry) & (key <= qry + blk)
    band_first = band & (key >= jnp.where(i > 0, 0, blk))
    ones_v = jnp.ones((2 * blk, LANES), BF16)

    def sub_block(g, dil, r, sub):
        rows_q = slice(sub * blk, (sub + 1) * blk)
        qb = q_refs[g][0, r, rows_q, :]
        if sub == 0:
            kcat = jnp.concatenate([kp_refs[g][0, r], k_refs[g][0, r, 0:blk, :]], axis=0)
            vcat = jnp.concatenate([vp_refs[g][0, r], v_refs[g][0, r, 0:blk, :]], axis=0)
            valid = band_first
        else:
            kcat = k_refs[g][0, r, (sub - 1) * blk:(sub + 1) * blk, :]
            vcat = v_refs[g][0, r, (sub - 1) * blk:(sub + 1) * blk, :]
            valid = band
        zq = jnp.zeros_like(qb)
        q2 = jnp.concatenate([jnp.where(lane_q, qb, zq), jnp.where(lane_q, zq, qb)], axis=0)
        s = lax.dot_general(q2, kcat, (((1,), (1,)), ((), ())), preferred_element_type=F32)
        s = jnp.where(valid, s, -jnp.inf)
        m = jnp.max(s, axis=-1, keepdims=True)
        p = jnp.exp(s - m).astype(BF16)
        pv = jnp.dot(p, jnp.concatenate([vcat, ones_v], axis=1), preferred_element_type=F32)
        acc = jnp.where(lane_q, pv[:blk, :LANES], pv[blk:, :LANES])
        den = jnp.where(lane_q, pv[:blk, LANES:], pv[blk:, LANES:])
        return acc, den, jnp.where(lane_q, m[:blk], m[blk:])

    last = N_GROUPS - 1
    big = ATTN_PATTERNS[last][1]
    assert big * blk == ATTN_TILE and all(dl <= big for _, dl in ATTN_PATTERNS)
    for g, (_, dil) in enumerate(ATTN_PATTERNS[:last]):
        nsub = ATTN_TILE // (dil * blk)

        def residue(r, carry, g=g, dil=dil, nsub=nsub):
            for sub in range(nsub):
                rows = pl.ds(pl.multiple_of(r * (ATTN_TILE // dil) + sub * blk, blk), blk)
                acc_ref[g, rows, :], l_ref[g, rows, :], m_ref[g, rows, :] = sub_block(g, dil, r, sub)
            return carry

        if dil == 1:
            residue(0, 0)
        else:
            lax.fori_loop(0, dil, residue, 0, unroll=min(dil, max(1, 8 // nsub)))

    def widest(rb, carry):
        parts = []
        for g, (_, dil) in enumerate(ATTN_PATTERNS[:last]):
            step = big // dil
            start = (rb % dil) * (ATTN_TILE // dil) + rb // dil
            rows = pl.ds(start, blk, stride=step) if step > 1 else pl.ds(pl.multiple_of(start, blk), blk)
            parts.append((acc_ref[g, rows, :], l_ref[g, rows, :], m_ref[g, rows, :]))
        parts.append(sub_block(last, big, rb, 0))
        mx = parts[0][2]
        for _, _, m in parts[1:]:
            mx = jnp.maximum(mx, m)
        num = jnp.zeros((blk, LANES), F32)
        den = jnp.zeros((blk, LANES), F32)
        for acc, l, m in parts:
            e = jnp.exp(m - mx)
            num = num + e * acc
            den = den + e * l
        out_ref[0, pl.ds(rb, blk, stride=big), :] = num / den
        return carry

    lax.fori_loop(0, big, widest, 0, unroll=8)


def _attention(qs, ks, vs):
    b = qs[0].shape[0]
    s = qs[0].shape[1] * qs[0].shape[2]
    n_i = s // ATTN_TILE
    n_p = D_MODEL // LANES
    blk = SUB_BLOCK
    cur_specs, prev_specs = [], []
    for _, dil in ATTN_PATTERNS:
        per = ATTN_TILE // dil
        cur_specs.append(pl.BlockSpec((1, dil, per, LANES), lambda bi, i, p: (bi, 0, i, p)))
        nsub = per // blk
        prev_specs.append(pl.BlockSpec((1, dil, blk, LANES),
                                       lambda bi, i, p, nsub=nsub: (bi, 0, jnp.maximum(i * nsub - 1, 0), p)))
    return pl.pallas_call(
        _attn_kernel,
        grid=(b, n_i, n_p),
        in_specs=cur_specs * 3 + prev_specs * 2,
        out_specs=pl.BlockSpec((1, ATTN_TILE, LANES), lambda bi, i, p: (bi, i, p)),
        out_shape=jax.ShapeDtypeStruct((b, s, D_MODEL), F32),
        scratch_shapes=[pltpu.VMEM((N_GROUPS - 1, ATTN_TILE, LANES), F32)] * 3,
        compiler_params=_params(("arbitrary", "arbitrary", "arbitrary")),
        name="dilated_attention",
    )(*qs, *ks, *vs, *ks, *vs)


def _outproj_kernel(h_ref, comb_ref, g_ref, wg_ref, wo_ref, gf_ref, out_ref, *, final):
    x = h_ref[0]
    hn = _rmsnorm(x, g_ref[...]).astype(BF16)
    gate = jnp.dot(hn, wg_ref[...], preferred_element_type=F32)
    y = comb_ref[0] * _silu(gate)
    o = x + jnp.dot(y.astype(BF16), wo_ref[...], preferred_element_type=F32)
    if final:
        o = _rmsnorm(o, gf_ref[...])
    out_ref[0] = o


def _outproj(h, comb, g, w_gate, w_out, g_final, *, final, tm, gather=None):
    b, s, d = h.shape
    kern = functools.partial(_outproj_kernel, final=final)
    tile = pl.BlockSpec((1, tm, d), lambda bi, i: (bi, i, 0))
    return _tile_call(
        kern, 6, 1, (h, comb, g, w_gate, w_out, g_final), gather,
        grid=(b, s // tm),
        in_specs=[tile, tile, _const_spec((1, d)), _const_spec((d, d)), _const_spec((d, d)), _const_spec((1, d))],
        out_specs=[tile],
        out_shape=[jax.ShapeDtypeStruct((b, s, d), F32)],
        scratch_shapes=[],
        name="gate_outproj",
    )


def _sattn_kernel(q_ref, c_ref, seg_ref, segt_ref, acc_ref, m_ref, l_ref, *, bb):
    n = c_ref.shape[1]
    q = q_ref[...].astype(BF16)
    k = c_ref[:, :, 0:D_MODEL]
    v = c_ref[:, :, D_MODEL:].astype(F32)
    prod = (k * q[:, None, :]).reshape(bb * n, D_MODEL)
    s = jnp.dot(prod, seg_ref[...], preferred_element_type=F32).reshape(bb, n, LANES)
    m = jnp.max(s, axis=1)
    p = jnp.exp(s - m[:, None, :])
    l = jnp.sum(p, axis=1)
    pe = jnp.dot(p.reshape(bb * n, LANES).astype(BF16), segt_ref[...], preferred_element_type=F32)
    acc_ref[...] = jnp.sum(pe.reshape(bb, n, D_MODEL) * v, axis=1)
    m_ref[...] = m
    l_ref[...] = l


def _sample_attention(q, rows, seg, segt, *, bb):
    n_b, n_rows, width = rows.shape
    kern = functools.partial(_sattn_kernel, bb=bb)
    return pl.pallas_call(
        kern,
        grid=(n_b // bb,),
        in_specs=[
            pl.BlockSpec((bb, D_MODEL), lambda i: (i, 0)),
            pl.BlockSpec((bb, n_rows, width), lambda i: (i, 0, 0)),
            _const_spec(seg.shape),
            _const_spec(segt.shape),
        ],
        out_specs=[
            pl.BlockSpec((bb, D_MODEL), lambda i: (i, 0)),
            pl.BlockSpec((bb, LANES), lambda i: (i, 0)),
            pl.BlockSpec((bb, LANES), lambda i: (i, 0)),
        ],
        out_shape=[
            jax.ShapeDtypeStruct((n_b, D_MODEL), F32),
            jax.ShapeDtypeStruct((n_b, LANES), F32),
            jax.ShapeDtypeStruct((n_b, LANES), F32),
        ],
        compiler_params=_params(("arbitrary",)),
        name="sample_attention",
    )(q, rows, seg, segt)


def _expand_heads(x, segt):
    x1 = x.astype(BF16)
    r1 = x - x1.astype(F32)
    x2 = r1.astype(BF16)
    x3 = (r1 - x2.astype(F32)).astype(BF16)
    out = jnp.dot(x1, segt, preferred_element_type=F32)
    out = out + jnp.dot(x2, segt, preferred_element_type=F32)
    return out + jnp.dot(x3, segt, preferred_element_type=F32)


def _soutproj_kernel(h_ref, q_ref, kvn0_ref, kvn1_ref, kvn2_ref, acc_ref, m_ref, l_ref, seg_ref, segt_ref,
                     g_ref, wg_ref, wo_ref, gf_ref, out_ref, *, final):
    kvn_refs = (kvn0_ref, kvn1_ref, kvn2_ref)
    seg, segt = seg_ref[...], segt_ref[...]
    s_new = []
    for g in range(N_GROUPS):
        qg = q_ref[:, g * D_MODEL:(g + 1) * D_MODEL]
        kn = kvn_refs[g][:, 0:D_MODEL]
        s_new.append(jnp.dot((qg * kn).astype(BF16), seg, preferred_element_type=F32))
    ms = [m_ref[g] for g in range(N_GROUPS)]
    mx = ms[0]
    for t in ms[1:] + s_new:
        mx = jnp.maximum(mx, t)
    e_old = [jnp.exp(ms[g] - mx) for g in range(N_GROUPS)]
    e_new = [jnp.exp(s_new[g] - mx) for g in range(N_GROUPS)]
    den = jnp.zeros_like(mx)
    for g in range(N_GROUPS):
        den = den + e_old[g] * l_ref[g] + e_new[g]
    comb = jnp.zeros(h_ref.shape, F32)
    for g in range(N_GROUPS):
        comb = comb + _expand_heads(e_old[g] / den, segt) * acc_ref[g]
        comb = comb + _expand_heads(e_new[g] / den, segt) * kvn_refs[g][:, D_MODEL:]

    x = h_ref[...]
    hn = _rmsnorm(x, g_ref[...]).astype(BF16)
    gate = jnp.dot(hn, wg_ref[...], preferred_element_type=F32)
    y = comb * _silu(gate)
    o = x + jnp.dot(y.astype(BF16), wo_ref[...], preferred_element_type=F32)
    if final:
        o = _rmsnorm(o, gf_ref[...])
    out_ref[...] = o


def _sample_outproj(h, q, kvn, acc, m, l, seg, segt, g, w_gate, w_out, g_final, *, final):
    kern = functools.partial(_soutproj_kernel, final=final)
    return pl.pallas_call(
        kern,
        out_shape=jax.ShapeDtypeStruct(h.shape, F32),
        compiler_params=pltpu.CompilerParams(vmem_limit_bytes=VMEM_LIMIT),
        name="gate_outproj_sample",
    )(h, q, *kvn, acc, m, l, seg, segt, g, w_gate, w_out, g_final)


def kernel(x_prompt, x_sample, state_pool, cache_kv_w128, cache_kv_w512, cache_kv_w2048, g_a, w_a_in, w_a_group,
           a_scale, w_a_out, g_kv, w_kv, g_b, w_b_in, w_b_out, g_final):
    b, s, d = x_prompt.shape
    n_s, t_s, _ = x_sample.shape
    assert d == D_MODEL and t_s == 1 and s % ATTN_TILE == 0
    n_pool = w_a_in.shape[0]
    n_attn = w_b_in.shape[0]
    caches = (cache_kv_w128, cache_kv_w512, cache_kv_w2048)
    dils = tuple(dl for _, dl in ATTN_PATTERNS)
    keeps = tuple(min(w, s) for w, _ in ATTN_PATTERNS)
    for c, dl in zip(caches, dils):
        assert c.shape[1] == SUB_BLOCK * dl
    tm = 256
    steps = b * (s // tm)

    slab_positions = 2048
    gathered, sels, cache_ts, rows_per_step, pending = [], [], [], [], []
    for ci, (c, dl) in enumerate(zip(caches, dils)):
        n_pos = c.shape[1]
        n_kv = c.shape[2]
        kept = n_pos // dl
        nb = max(1, min(slab_positions // n_pos, n_s * n_kv // steps))
        assert (n_s * n_kv) % (nb * steps) == 0
        cache_ts.append(jnp.transpose(c, (0, 2, 3, 4, 1)))
        sels.append((jnp.arange(n_pos, dtype=jnp.int32)[None, :]
                     == dl * jnp.arange(kept, dtype=jnp.int32)[:, None]).astype(BF16))
        rows_per_step.append(nb)
        n_calls = n_s * n_kv // (nb * steps)
        gathered.append(None if n_calls == 1 else jnp.zeros((n_s, kept, n_kv * D_MODEL), BF16))
        pending += [(ci, first) for first in range(0, n_calls * steps, steps)]
    pending.sort(key=lambda job: -caches[job[0]].shape[1])
    slots = ["pool"] * n_pool + ["kv"] + ["q", "out"] * n_attn
    by_preference = sorted(range(len(slots)), key=lambda k: ("pool", "q", "kv", "out").index(slots[k]))
    assert len(pending) <= len(slots)
    plan = [None] * len(slots)
    for k, job in zip(by_preference, pending):
        plan[k] = job
    plan = iter(plan)

    def hosted(fn, *args, **kwargs):
        job = next(plan)
        if job is None:
            return fn(*args, **kwargs)[0]
        ci, first = job
        outs, gathered[ci] = fn(*args, gather=_Gather(cache_ts[ci], sels[ci], rows_per_step[ci], first, gathered[ci]),
                                **kwargs)
        return outs

    wb = lambda a: a.astype(BF16)
    row = lambda a: a.reshape(1, -1)
    w_a_in_b, w_a_group_b, w_a_out_b = wb(w_a_in), wb(w_a_group), wb(w_a_out)
    w_kv_b, w_b_in_b, w_b_out_b = wb(w_kv), wb(w_b_in), wb(w_b_out)
    q_cols = N_GROUPS * D_MODEL

    tab_p = _rope_tables(jnp.arange(s, dtype=jnp.int32))
    tab_s = _rope_tables(jnp.broadcast_to(PAST_LEN + jnp.arange(t_s, dtype=jnp.int32), (n_s,)))
    head_of_lane = jnp.arange(D_MODEL, dtype=jnp.int32) // HEAD_DIM
    seg = (head_of_lane[:, None] == jnp.arange(LANES, dtype=jnp.int32)[None, :]).astype(BF16)
    segt = seg.T

    h_p = x_prompt
    h_s = x_sample.reshape(n_s, d)
    state_t = jnp.transpose(state_pool, (0, 2, 1, 3))
    pool_p, pool_s = [], []
    for li in range(n_pool):
        h_p, ulast = hosted(_pool_layer, h_p, row(g_a[li]), w_a_in_b[li], w_a_group_b[li], row(a_scale[li]),
                            w_a_out_b[li], tm=tm)
        pool_p.append(ulast[:, 16 - POOL_STATE:])
        h_s, nst = _spool_layer(h_s, state_t[li], row(g_a[li]), w_a_in_b[li],
                                w_a_group_b[li], row(a_scale[li]), w_a_out_b[li])
        pool_s.append(jnp.transpose(nst, (1, 0, 2)))

    kv_chunks = []
    for gi in range(N_GROUPS):
        kv_chunks.append((True, 1.0, dils[gi], True, (gi, 0)))
        kv_chunks.append((False, 1.0, dils[gi], True, (gi, D_MODEL)))
    kv_out = hosted(_proj, h_p, row(g_kv), w_kv_b, tab_p, tuple(kv_chunks), tuple((kp, 2 * D_MODEL) for kp in keeps), tm=tm)
    ks =[kv_out[2 * gi] for gi in range(N_GROUPS)]
    vs = [kv_out[2 * gi + 1] for gi in range(N_GROUPS)]
    kv_keep_p = kv_out[2 * N_GROUPS:]

    skv_chunks = []
    for gi in range(N_GROUPS):
        skv_chunks.append((True, 1.0, 1, False, (gi, 0)))
        skv_chunks.append((False, 1.0, 1, False, (gi, D_MODEL)))
    kvn, _ = _proj(h_s[None], row(g_kv), w_kv_b, tab_s, tuple(skv_chunks), ((n_s, 2 * D_MODEL),) * N_GROUPS, tm=n_s)
    kvn = [a[0] for a in kvn]

    q_scale = 1.0 / math.sqrt(HEAD_DIM)
    q_chunks = tuple((True, q_scale, dils[gi], True, None) for gi in range(N_GROUPS))
    sq_chunks = tuple((True, q_scale, 1, False, (0, gi * D_MODEL)) for gi in range(N_GROUPS))
    w_qs = [w_b_in_b[lj][:, :q_cols] for lj in range(n_attn)]
    w_gates = [w_b_in_b[lj][:, q_cols:] for lj in range(n_attn)]

    for lj in range(n_attn):
        final = lj == n_attn - 1
        qs = hosted(_proj, h_p, row(g_b[lj]), w_qs[lj], tab_p, q_chunks, (), tm=tm)
        comb = _attention(qs, ks, vs)
        (h_p,) = hosted(_outproj, h_p, comb, row(g_b[lj]), w_gates[lj], w_b_out_b[lj], row(g_final), final=final, tm=tm)
    assert next(plan, None) is None

    for lj in range(n_attn):
        final = lj == n_attn - 1
        w_q, w_gate = w_qs[lj], w_gates[lj]
        (q_s,), _ = _proj(h_s[None], row(g_b[lj]), w_q, tab_s, sq_chunks, ((n_s, q_cols),), tm=n_s)
        q_s = q_s[0]
        parts = [_sample_attention(q_s[:, gi * D_MODEL:(gi + 1) * D_MODEL], gathered[gi], seg, segt, bb=8)
                 for gi in range(N_GROUPS)]
        acc = jnp.stack([pt[0] for pt in parts])
        m = jnp.stack([pt[1] for pt in parts])
        l = jnp.stack([pt[2] for pt in parts])
        h_s = _sample_outproj(h_s, q_s, kvn, acc, m, l, seg, segt, row(g_b[lj]), w_gate, w_b_out_b[lj],
                              row(g_final), final=final)

    kv_shape = (2, N_HEADS, HEAD_DIM)
    outs = [h_p, h_s.reshape(n_s, t_s, d), jnp.stack(pool_p), jnp.stack(pool_s)]
    for gi in range(N_GROUPS):
        outs.append(kv_keep_p[gi].reshape(b, keeps[gi], *kv_shape))
        outs.append(kvn[gi].reshape(n_s, t_s, *kv_shape))
    return tuple(outs)
```

```python
import functools
import math
from typing import NamedTuple

import jax
import jax.numpy as jnp
from jax import lax
from jax.experimental import pallas as pl
from jax.experimental.pallas import tpu as pltpu

D_MODEL = 1024
HEAD_DIM = 64
N_HEADS = D_MODEL // HEAD_DIM
ROT_DIM = HEAD_DIM // 4
ROPE_THETA = 500000.0
POOL_WINDOWS = (2, 4, 8, 16)
POOL_GROUP_WIDTH = D_MODEL // len(POOL_WINDOWS)
POOL_STATE = max(POOL_WINDOWS) - 1
ATTN_PATTERNS = ((128, 1), (512, 4), (2048, 16))
N_GROUPS = len(ATTN_PATTERNS)
SUB_BLOCK = 128
RMS_EPS = 1e-6
PAST_LEN = 2048

LANES = 128
ATTN_TILE = SUB_BLOCK * max(d for _, d in ATTN_PATTERNS)
POOL_HIST = 32
VMEM_LIMIT = 56 * 1024 * 1024

F32 = jnp.float32
BF16 = jnp.bfloat16


def _rmsnorm(x, g):
    ms = jnp.mean(x * x, axis=-1, keepdims=True)
    return x * lax.rsqrt(ms + RMS_EPS) * g


def _silu(x):
    return x * (1.0 / (1.0 + jnp.exp(-x)))


def _const_spec(shape):
    return pl.BlockSpec(shape, lambda *_: (0,) * len(shape), pipeline_mode=pl.Buffered(1))


def _params(sem):
    return pltpu.CompilerParams(dimension_semantics=sem, vmem_limit_bytes=VMEM_LIMIT)


GATHER_FEATURES = 256


class _Gather(NamedTuple):
    cache_t: jax.Array
    sel: jax.Array
    rows_per_step: int
    first_step: int
    out: jax.Array | None


def _gather_body(x_ref, sel_ref, o_ref):
    nb, _, _, hd, n_pos = x_ref.shape
    heads = GATHER_FEATURES // hd
    sel = sel_ref[...]
    for bi in range(nb):
        for c in range(N_HEADS // heads):
            x = x_ref[bi, 0, heads * c:heads * (c + 1)].reshape(GATHER_FEATURES, n_pos).astype(BF16)
            y = lax.dot_general(sel, x, (((1,), (1,)), ((), ())), preferred_element_type=F32)
            o_ref[bi, :, c * GATHER_FEATURES:(c + 1) * GATHER_FEATURES] = y.astype(BF16)


GATHER_SLAB_POSITIONS = 2048
GATHER_HOST_PREFERENCE = ("attn", "pool", "q", "kv", "out")


def _plan_gathers(hosts, cache_positions, halves):
    plan = [None] * len(hosts)
    by_preference = sorted(range(len(hosts)), key=lambda k: GATHER_HOST_PREFERENCE.index(hosts[k][0]))
    for ci in sorted(range(len(cache_positions)), key=lambda c: -cache_positions[c]):
        nb = max(1, GATHER_SLAB_POSITIONS // cache_positions[ci])
        chosen = None
        while nb >= 1 and chosen is None:
            want, picked = halves // nb, []
            for k in by_preference:
                if plan[k] is None and hosts[k][1] <= want:
                    picked.append(k)
                    want -= hosts[k][1]
            if want == 0 and halves % nb == 0:
                chosen = sorted(picked)
            else:
                nb //= 2
        assert chosen is not None, "no set of hosting calls covers this cache"
        first = 0
        for k in chosen:
            plan[k] = (ci, first, nb)
            first += hosts[k][1]
    return plan


def _tile_call(body, n_in, n_out, args, gather, *, grid, in_specs, out_specs, out_shape, scratch_shapes, name):
    if gather is None:
        outs = pl.pallas_call(body, grid=grid, in_specs=in_specs, out_specs=out_specs, out_shape=out_shape,
                              scratch_shapes=scratch_shapes, compiler_params=_params(("arbitrary",) * len(grid)),
                              name=name)(*args)
        return list(outs), None
    nb = gather.rows_per_step
    n_kv, n_h, hd, n_pos = gather.cache_t.shape[1:]
    kept = gather.sel.shape[0]

    def slab(*idx):
        step = 0
        for size, k in zip(grid, idx):
            step = step * size + k
        step = gather.first_step + step
        return step // n_kv, step % n_kv

    side_in = [gather.cache_t, gather.sel] + ([] if gather.out is None else [gather.out])
    n_side = len(side_in)

    def kern(*refs):
        x_ref, sel_ref = refs[n_in:n_in + 2]
        o_ref = refs[n_in + n_side + n_out]
        _gather_body(x_ref, sel_ref, o_ref)
        body(*refs[:n_in], *refs[n_in + n_side:n_in + n_side + n_out], *refs[n_in + n_side + n_out + 1:])

    side_specs = [pl.BlockSpec((nb, 1, n_h, hd, n_pos), lambda *idx: (*slab(*idx), 0, 0, 0)),
                  _const_spec(gather.sel.shape)]
    if gather.out is not None:
        side_specs.append(pl.BlockSpec(memory_space=pl.ANY))
    outs = pl.pallas_call(
        kern,
        grid=grid,
        in_specs=list(in_specs) + side_specs,
        out_specs=list(out_specs) + [pl.BlockSpec((nb, kept, n_h * hd), lambda *idx: (slab(*idx)[0], 0, slab(*idx)[1]))],
        out_shape=list(out_shape) + [jax.ShapeDtypeStruct((gather.cache_t.shape[0], kept, n_kv * n_h * hd), BF16)],
        scratch_shapes=scratch_shapes,
        input_output_aliases={} if gather.out is None else {n_in + 2: n_out},
        compiler_params=_params(("arbitrary",) * len(grid)),
        name=name,
    )(*args, *side_in)
    return list(outs[:n_out]), outs[n_out]


def _pool_kernel(h_ref, g_ref, win_ref, wgrp_ref, scale_ref, wout_ref, out_ref, ulast_ref,
                 ext_ref, lvl_ref, *, tm):
    i = pl.program_id(1)
    hist = POOL_HIST

    @pl.when(i == 0)
    def _():
        ext_ref[0:hist, :] = jnp.zeros((hist, D_MODEL), F32)

    x = h_ref[0]
    hn = _rmsnorm(x, g_ref[...]).astype(BF16)
    proj = jnp.dot(hn, win_ref[...], preferred_element_type=F32)
    gate = proj[:, D_MODEL:]
    ext_ref[hist:hist + tm, :] = proj[:, :D_MODEL]

    total = hist + tm
    pos = i * tm + lax.broadcasted_iota(jnp.int32, (tm, 1), 0)
    src = ext_ref
    z_parts = []
    for k, w in enumerate(POOL_WINDOWS):
        shift = w // 2
        lo = 8 * (k + 1)
        c0 = k * POOL_GROUP_WIDTH
        cur = src[lo:total, c0:] + src[lo - shift:total - shift, c0:]
        dst = lvl_ref.at[k % 2]
        dst[lo:total, c0:] = cur
        inv_cnt = 1.0 / jnp.minimum(pos + 1, w).astype(F32)
        wsum = dst[hist:total, c0:c0 + POOL_GROUP_WIDTH]
        r = wsum * inv_cnt - ext_ref[hist:total, c0:c0 + POOL_GROUP_WIDTH]
        z_parts.append(jnp.dot(r.astype(BF16), wgrp_ref[k], preferred_element_type=F32))
        src = dst
    z = jnp.concatenate(z_parts, axis=-1)
    y = z * scale_ref[...] * _silu(gate)
    out_ref[0] = x + jnp.dot(y.astype(BF16), wout_ref[...], preferred_element_type=F32)

    ulast_ref[0] = ext_ref[hist + tm - 16:hist + tm, :]
    ext_ref[0:hist, :] = ext_ref[tm:tm + hist, :]


def _pool_layer(h, g, w_in, w_grp, scale, w_out, *, tm, gather=None):
    b, s, d = h.shape
    kern = functools.partial(_pool_kernel, tm=tm)
    return _tile_call(
        kern, 6, 2, (h, g, w_in, w_grp, scale, w_out), gather,
        grid=(b, s // tm),
        in_specs=[
            pl.BlockSpec((1, tm, d), lambda bi, i: (bi, i, 0)),
            _const_spec((1, d)),
            _const_spec((d, 2 * d)),
            _const_spec((len(POOL_WINDOWS), POOL_GROUP_WIDTH, POOL_GROUP_WIDTH)),
            _const_spec((1, d)),
            _const_spec((d, d)),
        ],
        out_specs=[
            pl.BlockSpec((1, tm, d), lambda bi, i: (bi, i, 0)),
            pl.BlockSpec((1, 16, d), lambda bi, i: (bi, 0, 0)),
        ],
        out_shape=[jax.ShapeDtypeStruct((b, s, d), F32), jax.ShapeDtypeStruct((b, 16, d), F32)],
        scratch_shapes=[pltpu.VMEM((POOL_HIST + tm, d), F32), pltpu.VMEM((2, POOL_HIST + tm, d), F32)],
        name="pool_layer",
    )


def _spool_kernel(h_ref, st_ref, g_ref, win_ref, wgrp_ref, scale_ref, wout_ref, out_ref, nst_ref):
    x = h_ref[...]
    hn = _rmsnorm(x, g_ref[...]).astype(BF16)
    proj = jnp.dot(hn, win_ref[...], preferred_element_type=F32)
    u = proj[:, :D_MODEL]
    gate = proj[:, D_MODEL:]
    z_parts = []
    for k, w in enumerate(POOL_WINDOWS):
        c0 = k * POOL_GROUP_WIDTH
        uk = u[:, c0:c0 + POOL_GROUP_WIDTH]
        tot = uk
        for j in range(1, w):
            tot = tot + st_ref[POOL_STATE - j, :, c0:c0 + POOL_GROUP_WIDTH]
        cnt = float(min(PAST_LEN + 1, w))
        r = tot / cnt - uk
        z_parts.append(jnp.dot(r.astype(BF16), wgrp_ref[k], preferred_element_type=F32))
    z = jnp.concatenate(z_parts, axis=-1)
    y = z * scale_ref[...] * _silu(gate)
    out_ref[...] = x + jnp.dot(y.astype(BF16), wout_ref[...], preferred_element_type=F32)
    nst_ref[0:POOL_STATE - 1] = st_ref[1:POOL_STATE]
    nst_ref[POOL_STATE - 1] = u


def _spool_layer(h, state_t, g, w_in, w_grp, scale, w_out):
    n, d = h.shape
    return pl.pallas_call(
        _spool_kernel,
        out_shape=[jax.ShapeDtypeStruct((n, d), F32), jax.ShapeDtypeStruct(state_t.shape, F32)],
        compiler_params=pltpu.CompilerParams(vmem_limit_bytes=VMEM_LIMIT),
        name="pool_layer_sample",
    )(h, state_t, g, w_in, w_grp, scale, w_out)


def _rope_tables(pos):
    half = ROT_DIM // 2
    inv = 1.0 / (ROPE_THETA ** (jnp.arange(0, ROT_DIM, 2, dtype=jnp.float32) / ROT_DIM))
    ang = pos.astype(jnp.float32)[:, None] * inv[None, :]
    cos, sin = jnp.cos(ang), jnp.sin(ang)
    t = pos.shape[0]
    rest = HEAD_DIM - ROT_DIM
    cos_h = jnp.concatenate([cos, cos, jnp.ones((t, rest), F32)], axis=1)
    sin_a = jnp.concatenate([jnp.zeros((t, half), F32), sin, jnp.zeros((t, rest), F32)], axis=1)
    sin_b = jnp.concatenate([-sin, jnp.zeros((t, half + rest), F32)], axis=1)
    reps = LANES // HEAD_DIM
    return tuple(jnp.tile(a, (1, reps)) for a in (cos_h, sin_a, sin_b))


def _proj_kernel(h_ref, g_ref, w_ref, cos_ref, sa_ref, sb_ref, *rest, tm, chunks, f32_outs, n_bf16):
    bf_refs = rest[:n_bf16]
    f32_refs = rest[n_bf16:n_bf16 + len(f32_outs)]
    y_ref = rest[-1]
    half = ROT_DIM // 2

    hn = _rmsnorm(h_ref[0], g_ref[...]).astype(BF16)
    bi = 0
    n_lt = D_MODEL // LANES
    for c, (rope, scale, dil, bf_out, f32_slot) in enumerate(chunks):
        y = jnp.dot(hn, w_ref[:, c * D_MODEL:(c + 1) * D_MODEL], preferred_element_type=F32)
        for lt in range(n_lt):
            ys = y[:, lt * LANES:(lt + 1) * LANES]
            if rope:
                ys = (ys * cos_ref[...] + pltpu.roll(ys, half, 1) * sa_ref[...]
                      + pltpu.roll(ys, LANES - half, 1) * sb_ref[...])
            y_ref[lt] = ys * scale if scale != 1.0 else ys
        if f32_slot is not None:
            oi, off = f32_slot
            kb = min(tm, f32_outs[oi][0])
            for lt in range(n_lt):
                f32_refs[oi][0, :, off + lt * LANES:off + (lt + 1) * LANES] = y_ref[lt, tm - kb:tm, :]
        if bf_out:
            o_ref = bf_refs[bi]
            bi += 1
            for lt in range(n_lt):
                cols = slice(lt * LANES, (lt + 1) * LANES)
                if dil == 1:
                    o_ref[0, 0, :, cols] = y_ref[lt].astype(BF16)
                else:
                    for r in range(dil):
                        o_ref[0, r, :, cols] = y_ref[lt, pl.ds(r, tm // dil, stride=dil), :].astype(BF16)


def _proj(h, g, w, tables, chunks, f32_outs, *, tm, gather=None):
    b, s, d = h.shape
    n_i = s // tm
    out_shapes, out_specs = [], []
    for rope, scale, dil, bf_out, f32_slot in chunks:
        if bf_out:
            out_shapes.append(jax.ShapeDtypeStruct((b, dil, s // dil, d), BF16))
            out_specs.append(pl.BlockSpec((1, dil, tm // dil, d), lambda bi, i: (bi, 0, i, 0)))
    n_bf16 = len(out_shapes)
    for keep, width in f32_outs:
        kb = min(tm, keep)
        first = n_i - keep // kb
        out_shapes.append(jax.ShapeDtypeStruct((b, keep, width), F32))
        out_specs.append(pl.BlockSpec((1, kb, width), lambda bi, i, first=first: (bi, jnp.maximum(i - first, 0), 0)))
    kern = functools.partial(_proj_kernel, tm=tm, chunks=chunks, f32_outs=f32_outs, n_bf16=n_bf16)
    tab_spec = pl.BlockSpec((tm, LANES), lambda bi, i: (i, 0))
    return _tile_call(
        kern, 6, len(out_shapes), (h, g, w, *tables), gather,
        grid=(b, n_i),
        in_specs=[
            pl.BlockSpec((1, tm, d), lambda bi, i: (bi, i, 0)),
            _const_spec((1, d)),
            _const_spec((d, len(chunks) * D_MODEL)),
            tab_spec, tab_spec, tab_spec,
        ],
        out_specs=out_specs,
        out_shape=out_shapes,
        scratch_shapes=[pltpu.VMEM((d // LANES, tm, LANES), F32)],
        name="proj_rope",
    )


def _attn_kernel(*refs):
    q_refs = refs[0:3]
    k_refs = refs[3:6]
    v_refs = refs[6:9]
    kp_refs = refs[9:12]
    vp_refs = refs[12:15]
    out_ref = refs[15]
    acc_ref, m_ref, l_ref = refs[16:19]
    i = pl.program_id(1)
    blk = SUB_BLOCK

    lane_q = lax.broadcasted_iota(jnp.int32, (blk, LANES), 1) < HEAD_DIM
    key = lax.broadcasted_iota(jnp.int32, (2 * blk, 2 * blk), 1)
    qry = lax.broadcasted_iota(jnp.int32, (2 * blk, 2 * blk), 0) % blk
    band = (key >= qry) & (key <= qry + blk)
    band_first = band & (key >= jnp.where(i > 0, 0, blk))
    ones_v = jnp.ones((2 * blk, LANES), BF16)

    def sub_block(g, dil, r, sub):
        rows_q = slice(sub * blk, (sub + 1) * blk)
        qb = q_refs[g][0, r, rows_q, :]
        if sub == 0:
            kcat = jnp.concatenate([kp_refs[g][0, r], k_refs[g][0, r, 0:blk, :]], axis=0)
            vcat = jnp.concatenate([vp_refs[g][0, r], v_refs[g][0, r, 0:blk, :]], axis=0)
            valid = band_first
        else:
            kcat = k_refs[g][0, r, (sub - 1) * blk:(sub + 1) * blk, :]
            vcat = v_refs[g][0, r, (sub - 1) * blk:(sub + 1) * blk, :]
            valid = band
        zq = jnp.zeros_like(qb)
        q2 = jnp.concatenate([jnp.where(lane_q, qb, zq), jnp.where(lane_q, zq, qb)], axis=0)
        s = lax.dot_general(q2, kcat, (((1,), (1,)), ((), ())), preferred_element_type=F32)
        s = jnp.where(valid, s, -jnp.inf)
        m = jnp.max(s, axis=-1, keepdims=True)
        p = jnp.exp(s - m).astype(BF16)
        pv = jnp.dot(p, jnp.concatenate([vcat, ones_v], axis=1), preferred_element_type=F32)
        acc = jnp.where(lane_q, pv[:blk, :LANES], pv[blk:, :LANES])
        den = jnp.where(lane_q, pv[:blk, LANES:], pv[blk:, LANES:])
        return acc, den, jnp.where(lane_q, m[:blk], m[blk:])

    last = N_GROUPS - 1
    big = ATTN_PATTERNS[last][1]
    assert big * blk == ATTN_TILE and all(dl <= big for _, dl in ATTN_PATTERNS)
    for g, (_, dil) in enumerate(ATTN_PATTERNS[:last]):
        nsub = ATTN_TILE // (dil * blk)

        def residue(r, carry, g=g, dil=dil, nsub=nsub):
            for sub in range(nsub):
                rows = pl.ds(pl.multiple_of(r * (ATTN_TILE // dil) + sub * blk, blk), blk)
                acc_ref[g, rows, :], l_ref[g, rows, :], m_ref[g, rows, :] = sub_block(g, dil, r, sub)
            return carry

        if dil == 1:
            residue(0, 0)
        else:
            lax.fori_loop(0, dil, residue, 0, unroll=min(dil, max(1, 8 // nsub)))

    def widest(rb, carry):
        parts = []
        for g, (_, dil) in enumerate(ATTN_PATTERNS[:last]):
            step = big // dil
            start = (rb % dil) * (ATTN_TILE // dil) + rb // dil
            rows = pl.ds(start, blk, stride=step) if step > 1 else pl.ds(pl.multiple_of(start, blk), blk)
            parts.append((acc_ref[g, rows, :], l_ref[g, rows, :], m_ref[g, rows, :]))
        parts.append(sub_block(last, big, rb, 0))
        mx = parts[0][2]
        for _, _, m in parts[1:]:
            mx = jnp.maximum(mx, m)
        num = jnp.zeros((blk, LANES), F32)
        den = jnp.zeros((blk, LANES), F32)
        for acc, l, m in parts:
            e = jnp.exp(m - mx)
            num = num + e * acc
            den = den + e * l
        out_ref[0, pl.ds(rb, blk, stride=big), :] = num / den
        return carry

    lax.fori_loop(0, big, widest, 0, unroll=8)


def _attention(qs, ks, vs, *, gather=None):
    b = qs[0].shape[0]
    s = qs[0].shape[1] * qs[0].shape[2]
    n_i = s // ATTN_TILE
    n_p = D_MODEL // LANES
    blk = SUB_BLOCK
    cur_specs, prev_specs = [], []
    for _, dil in ATTN_PATTERNS:
        per = ATTN_TILE // dil
        cur_specs.append(pl.BlockSpec((1, dil, per, LANES), lambda bi, i, p: (bi, 0, i, p)))
        nsub = per // blk
        prev_specs.append(pl.BlockSpec((1, dil, blk, LANES),
                                       lambda bi, i, p, nsub=nsub: (bi, 0, jnp.maximum(i * nsub - 1, 0), p)))
    return _tile_call(
        _attn_kernel, 5 * N_GROUPS, 1, (*qs, *ks, *vs, *ks, *vs), gather,
        grid=(b, n_i, n_p),
        in_specs=cur_specs * 3 + prev_specs * 2,
        out_specs=[pl.BlockSpec((1, ATTN_TILE, LANES), lambda bi, i, p: (bi, i, p))],
        out_shape=[jax.ShapeDtypeStruct((b, s, D_MODEL), F32)],
        scratch_shapes=[pltpu.VMEM((N_GROUPS - 1, ATTN_TILE, LANES), F32)] * 3,
        name="dilated_attention",
    )


def _outproj_kernel(h_ref, comb_ref, g_ref, wg_ref, wo_ref, gf_ref, out_ref, *, final):
    x = h_ref[0]
    hn = _rmsnorm(x, g_ref[...]).astype(BF16)
    gate = jnp.dot(hn, wg_ref[...], preferred_element_type=F32)
    y = comb_ref[0] * _silu(gate)
    o = x + jnp.dot(y.astype(BF16), wo_ref[...], preferred_element_type=F32)
    if final:
        o = _rmsnorm(o, gf_ref[...])
    out_ref[0] = o


def _outproj(h, comb, g, w_in, w_out, g_final, *, final, tm, gather=None):
    b, s, d = h.shape
    kern = functools.partial(_outproj_kernel, final=final)
    tile = pl.BlockSpec((1, tm, d), lambda bi, i: (bi, i, 0))
    gate_spec = pl.BlockSpec((d, d), lambda *_: (0, w_in.shape[1] // d - 1), pipeline_mode=pl.Buffered(1))
    return _tile_call(
        kern, 6, 1, (h, comb, g, w_in, w_out, g_final), gather,
        grid=(b, s // tm),
        in_specs=[tile, tile, _const_spec((1, d)), gate_spec, _const_spec((d, d)), _const_spec((1, d))],
        out_specs=[tile],
        out_shape=[jax.ShapeDtypeStruct((b, s, d), F32)],
        scratch_shapes=[],
        name="gate_outproj",
    )


def _sattn_kernel(q_ref, c_ref, seg_ref, segt_ref, acc_ref, m_ref, l_ref, *, bb):
    n = c_ref.shape[1]
    part = 8
    for r0 in range(0, bb, part):
        rows = slice(r0, r0 + part)
        q = q_ref[rows, :].astype(BF16)
        k = c_ref[rows, :, 0:D_MODEL]
        v = c_ref[rows, :, D_MODEL:].astype(F32)
        prod = (k * q[:, None, :]).reshape(part * n, D_MODEL)
        s = jnp.dot(prod, seg_ref[...], preferred_element_type=F32).reshape(part, n, LANES)
        m = jnp.max(s, axis=1)
        p = jnp.exp(s - m[:, None, :])
        l = jnp.sum(p, axis=1)
        pe = jnp.dot(p.reshape(part * n, LANES).astype(BF16), segt_ref[...], preferred_element_type=F32)
        acc_ref[rows, :] = jnp.sum(pe.reshape(part, n, D_MODEL) * v, axis=1)
        m_ref[rows, :] = m
        l_ref[rows, :] = l


def _sample_attention(q, rows, seg, segt, *, bb):
    n_b, n_rows, width = rows.shape
    kern = functools.partial(_sattn_kernel, bb=bb)
    return pl.pallas_call(
        kern,
        grid=(n_b // bb,),
        in_specs=[
            pl.BlockSpec((bb, D_MODEL), lambda i: (i, 0)),
            pl.BlockSpec((bb, n_rows, width), lambda i: (i, 0, 0)),
            _const_spec(seg.shape),
            _const_spec(segt.shape),
        ],
        out_specs=[
            pl.BlockSpec((bb, D_MODEL), lambda i: (i, 0)),
            pl.BlockSpec((bb, LANES), lambda i: (i, 0)),
            pl.BlockSpec((bb, LANES), lambda i: (i, 0)),
        ],
        out_shape=[
            jax.ShapeDtypeStruct((n_b, D_MODEL), F32),
            jax.ShapeDtypeStruct((n_b, LANES), F32),
            jax.ShapeDtypeStruct((n_b, LANES), F32),
        ],
        compiler_params=_params(("arbitrary",)),
        name="sample_attention",
    )(q, rows, seg, segt)


def _expand_heads(x, segt):
    x1 = x.astype(BF16)
    r1 = x - x1.astype(F32)
    x2 = r1.astype(BF16)
    x3 = (r1 - x2.astype(F32)).astype(BF16)
    out = jnp.dot(x1, segt, preferred_element_type=F32)
    out = out + jnp.dot(x2, segt, preferred_element_type=F32)
    return out + jnp.dot(x3, segt, preferred_element_type=F32)


def _soutproj_kernel(h_ref, q_ref, kvn0_ref, kvn1_ref, kvn2_ref, acc_ref, m_ref, l_ref, seg_ref, segt_ref,
                     g_ref, wg_ref, wo_ref, gf_ref, out_ref, *, final):
    kvn_refs = (kvn0_ref, kvn1_ref, kvn2_ref)
    seg, segt = seg_ref[...], segt_ref[...]
    s_new = []
    for g in range(N_GROUPS):
        qg = q_ref[:, g * D_MODEL:(g + 1) * D_MODEL]
        kn = kvn_refs[g][:, 0:D_MODEL]
        s_new.append(jnp.dot((qg * kn).astype(BF16), seg, preferred_element_type=F32))
    ms = [m_ref[g] for g in range(N_GROUPS)]
    mx = ms[0]
    for t in ms[1:] + s_new:
        mx = jnp.maximum(mx, t)
    e_old = [jnp.exp(ms[g] - mx) for g in range(N_GROUPS)]
    e_new = [jnp.exp(s_new[g] - mx) for g in range(N_GROUPS)]
    den = jnp.zeros_like(mx)
    for g in range(N_GROUPS):
        den = den + e_old[g] * l_ref[g] + e_new[g]
    comb = jnp.zeros(h_ref.shape, F32)
    for g in range(N_GROUPS):
        comb = comb + _expand_heads(e_old[g] / den, segt) * acc_ref[g]
        comb = comb + _expand_heads(e_new[g] / den, segt) * kvn_refs[g][:, D_MODEL:]

    x = h_ref[...]
    hn = _rmsnorm(x, g_ref[...]).astype(BF16)
    gate = jnp.dot(hn, wg_ref[...], preferred_element_type=F32)
    y = comb * _silu(gate)
    o = x + jnp.dot(y.astype(BF16), wo_ref[...], preferred_element_type=F32)
    if final:
        o = _rmsnorm(o, gf_ref[...])
    out_ref[...] = o


def _sample_outproj(h, q, kvn, acc, m, l, seg, segt, g, w_gate, w_out, g_final, *, final):
    kern = functools.partial(_soutproj_kernel, final=final)
    return pl.pallas_call(
        kern,
        out_shape=jax.ShapeDtypeStruct(h.shape, F32),
        compiler_params=pltpu.CompilerParams(vmem_limit_bytes=VMEM_LIMIT),
        name="gate_outproj_sample",
    )(h, q, *kvn, acc, m, l, seg, segt, g, w_gate, w_out, g_final)


def kernel(x_prompt, x_sample, state_pool, cache_kv_w128, cache_kv_w512, cache_kv_w2048, g_a, w_a_in, w_a_group,
           a_scale, w_a_out, g_kv, w_kv, g_b, w_b_in, w_b_out, g_final):
    b, s, d = x_prompt.shape
    n_s, t_s, _ = x_sample.shape
    assert d == D_MODEL and t_s == 1 and s % ATTN_TILE == 0
    n_pool = w_a_in.shape[0]
    n_attn = w_b_in.shape[0]
    caches = (cache_kv_w128, cache_kv_w512, cache_kv_w2048)
    dils = tuple(dl for _, dl in ATTN_PATTERNS)
    keeps = tuple(min(w, s) for w, _ in ATTN_PATTERNS)
    for c, dl in zip(caches, dils):
        assert c.shape[1] == SUB_BLOCK * dl
    tm = 512
    tm_kv = 256

    hosts = ([("pool", b * (s // tm))] * n_pool + [("kv", b * (s // tm_kv))]
             + [("q", b * (s // tm)), ("attn", b * (s // ATTN_TILE) * (D_MODEL // LANES)), ("out", b * (s // tm))] * n_attn)
    plan = _plan_gathers(hosts, [c.shape[1] for c in caches], n_s * caches[0].shape[2])
    cache_ts = [jnp.transpose(c, (0, 2, 3, 4, 1)) for c in caches]
    sels, gathered = [], []
    for ci, (c, dl) in enumerate(zip(caches, dils)):
        n_pos, kept = c.shape[1], c.shape[1] // dl
        sels.append((jnp.arange(n_pos, dtype=jnp.int32)[None, :]
                     == dl * jnp.arange(kept, dtype=jnp.int32)[:, None]).astype(BF16))
        n_calls = sum(1 for job in plan if job is not None and job[0] == ci)
        assert n_calls >= 1
        gathered.append(None if n_calls == 1 else jnp.zeros((n_s, kept, c.shape[2] * D_MODEL), BF16))
    plan = iter(plan)

    def hosted(fn, *args, **kwargs):
        job = next(plan)
        if job is None:
            return fn(*args, **kwargs)[0]
        ci, first, nb = job
        outs, gathered[ci] = fn(*args, gather=_Gather(cache_ts[ci], sels[ci], nb, first, gathered[ci]), **kwargs)
        return outs

    wb = lambda a: a.astype(BF16)
    row = lambda a: a.reshape(1, -1)
    w_a_in_b, w_a_group_b, w_a_out_b = wb(w_a_in), wb(w_a_group), wb(w_a_out)
    w_kv_b, w_b_in_b, w_b_out_b = wb(w_kv), wb(w_b_in), wb(w_b_out)
    q_cols = N_GROUPS * D_MODEL

    tab_p = _rope_tables(jnp.arange(s, dtype=jnp.int32))
    tab_s = _rope_tables(jnp.broadcast_to(PAST_LEN + jnp.arange(t_s, dtype=jnp.int32), (n_s,)))
    head_of_lane = jnp.arange(D_MODEL, dtype=jnp.int32) // HEAD_DIM
    seg = (head_of_lane[:, None] == jnp.arange(LANES, dtype=jnp.int32)[None, :]).astype(BF16)
    segt = seg.T

    h_p = x_prompt
    h_s = x_sample.reshape(n_s, d)
    state_t = jnp.transpose(state_pool, (0, 2, 1, 3))
    pool_p, pool_s = [], []
    for li in range(n_pool):
        h_p, ulast = hosted(_pool_layer, h_p, row(g_a[li]), w_a_in_b[li], w_a_group_b[li], row(a_scale[li]),
                            w_a_out_b[li], tm=tm)
        pool_p.append(ulast[:, 16 - POOL_STATE:])
        h_s, nst = _spool_layer(h_s, state_t[li], row(g_a[li]), w_a_in_b[li],
                                w_a_group_b[li], row(a_scale[li]), w_a_out_b[li])
        pool_s.append(jnp.transpose(nst, (1, 0, 2)))

    kv_chunks = []
    for gi in range(N_GROUPS):
        kv_chunks.append((True, 1.0, dils[gi], True, (gi, 0)))
        kv_chunks.append((False, 1.0, dils[gi], True, (gi, D_MODEL)))
    kv_out = hosted(_proj, h_p, row(g_kv), w_kv_b, tab_p, tuple(kv_chunks), tuple((kp, 2 * D_MODEL) for kp in keeps), tm=tm_kv)
    ks =[kv_out[2 * gi] for gi in range(N_GROUPS)]
    vs = [kv_out[2 * gi + 1] for gi in range(N_GROUPS)]
    kv_keep_p = kv_out[2 * N_GROUPS:]

    skv_chunks = []
    for gi in range(N_GROUPS):
        skv_chunks.append((True, 1.0, 1, False, (gi, 0)))
        skv_chunks.append((False, 1.0, 1, False, (gi, D_MODEL)))
    kvn, _ = _proj(h_s[None], row(g_kv), w_kv_b, tab_s, tuple(skv_chunks), ((n_s, 2 * D_MODEL),) * N_GROUPS, tm=n_s)
    kvn = [a[0] for a in kvn]

    q_scale = 1.0 / math.sqrt(HEAD_DIM)
    q_chunks = tuple((True, q_scale, dils[gi], True, None) for gi in range(N_GROUPS))
    sq_chunks = tuple((True, q_scale, 1, False, (0, gi * D_MODEL)) for gi in range(N_GROUPS))
    for lj in range(n_attn):
        final = lj == n_attn - 1
        qs = hosted(_proj, h_p, row(g_b[lj]), w_b_in_b[lj], tab_p, q_chunks, (), tm=tm)
        (comb,) = hosted(_attention, qs, ks, vs)
        (h_p,) = hosted(_outproj, h_p, comb, row(g_b[lj]), w_b_in_b[lj], w_b_out_b[lj], row(g_final), final=final, tm=tm)
    assert next(plan, None) is None

    for lj in range(n_attn):
        final = lj == n_attn - 1
        w_gate = w_b_in_b[lj][:, q_cols:]
        (q_s,), _ = _proj(h_s[None], row(g_b[lj]), w_b_in_b[lj], tab_s, sq_chunks, ((n_s, q_cols),), tm=n_s)
        q_s = q_s[0]
        parts = [_sample_attention(q_s[:, gi * D_MODEL:(gi + 1) * D_MODEL], gathered[gi], seg, segt, bb=min(n_s, 16))
                 for gi in range(N_GROUPS)]
        acc = jnp.stack([pt[0] for pt in parts])
        m = jnp.stack([pt[1] for pt in parts])
        l = jnp.stack([pt[2] for pt in parts])
        h_s = _sample_outproj(h_s, q_s, kvn, acc, m, l, seg, segt, row(g_b[lj]), w_gate, w_b_out_b[lj],
                              row(g_final), final=final)

    kv_shape = (2, N_HEADS, HEAD_DIM)
    outs = [h_p, h_s.reshape(n_s, t_s, d), jnp.stack(pool_p), jnp.stack(pool_s)]
    for gi in range(N_GROUPS):
        outs.append(kv_keep_p[gi].reshape(b, keeps[gi], *kv_shape))
        outs.append(kvn[gi].reshape(n_s, t_s, *kv_shape))
    return tuple(outs)
```

```python
import functools
import math
from typing import NamedTuple

import jax
import jax.numpy as jnp
from jax import lax
from jax.experimental import pallas as pl
from jax.experimental.pallas import tpu as pltpu

D_MODEL = 1024
HEAD_DIM = 64
N_HEADS = D_MODEL // HEAD_DIM
ROT_DIM = HEAD_DIM // 4
ROPE_THETA = 500000.0
POOL_WINDOWS = (2, 4, 8, 16)
POOL_GROUP_WIDTH = D_MODEL // len(POOL_WINDOWS)
POOL_STATE = max(POOL_WINDOWS) - 1
ATTN_PATTERNS = ((128, 1), (512, 4), (2048, 16))
N_GROUPS = len(ATTN_PATTERNS)
SUB_BLOCK = 128
RMS_EPS = 1e-6
PAST_LEN = 2048

LANES = 128
ATTN_TILE = SUB_BLOCK * max(d for _, d in ATTN_PATTERNS)
POOL_HIST = 32
VMEM_LIMIT = 56 * 1024 * 1024

F32 = jnp.float32
BF16 = jnp.bfloat16


def _rmsnorm(x, g):
    ms = jnp.mean(x * x, axis=-1, keepdims=True)
    return x * lax.rsqrt(ms + RMS_EPS) * g


def _silu(x):
    return x * (1.0 / (1.0 + jnp.exp(-x)))


def _const_spec(shape):
    return pl.BlockSpec(shape, lambda *_: (0,) * len(shape), pipeline_mode=pl.Buffered(1))


def _params(sem):
    return pltpu.CompilerParams(dimension_semantics=sem, vmem_limit_bytes=VMEM_LIMIT)


GATHER_FEATURES = 256


class _Gather(NamedTuple):
    cache_t: jax.Array
    sel: jax.Array
    rows_per_step: int
    first_step: int
    out: jax.Array | None


GATHER_TRANSPOSE_MAX_POSITIONS = 512


def _gather_body(x_ref, sel_ref, o_ref, t_ref):
    nb, _, _, hd, n_pos = x_ref.shape
    kept = o_ref.shape[1]
    if n_pos > GATHER_TRANSPOSE_MAX_POSITIONS:
        heads = GATHER_FEATURES // hd
        sel = sel_ref[...]
        for bi in range(nb):
            for c in range(N_HEADS // heads):
                x = x_ref[bi, 0, heads * c:heads * (c + 1)].reshape(GATHER_FEATURES, n_pos).astype(BF16)
                y = lax.dot_general(sel, x, (((1,), (1,)), ((), ())), preferred_element_type=F32)
                o_ref[bi, :, c * GATHER_FEATURES:(c + 1) * GATHER_FEATURES] = y.astype(BF16)
        return
    dil = n_pos // kept
    pair = LANES // hd
    slot = 0
    for bi in range(nb):
        for c in range(N_HEADS // pair):
            x = x_ref[bi, 0, pair * c:pair * (c + 1)].reshape(LANES, n_pos)
            t_ref[slot] = x.T
            rows = t_ref[slot, pl.ds(0, kept, stride=dil), :] if dil > 1 else t_ref[slot]
            o_ref[bi, :, c * LANES:(c + 1) * LANES] = rows.astype(BF16)
            slot = 1 - slot


GATHER_SLAB_POSITIONS = 2048
GATHER_HOST_PREFERENCE = ("attn", "pool", "q", "kv", "out")


def _plan_gathers(hosts, cache_positions, halves):
    plan = [None] * len(hosts)
    by_preference = sorted(range(len(hosts)), key=lambda k: GATHER_HOST_PREFERENCE.index(hosts[k][0]))
    for ci in sorted(range(len(cache_positions)), key=lambda c: -cache_positions[c]):
        nb = max(1, GATHER_SLAB_POSITIONS // cache_positions[ci])
        chosen = None
        while nb >= 1 and chosen is None:
            want, picked = halves // nb, []
            for k in by_preference:
                if plan[k] is None and hosts[k][1] <= want:
                    picked.append(k)
                    want -= hosts[k][1]
            if want == 0 and halves % nb == 0:
                chosen = sorted(picked)
            else:
                nb //= 2
        assert chosen is not None, "no set of hosting calls covers this cache"
        first = 0
        for k in chosen:
            plan[k] = (ci, first, nb)
            first += hosts[k][1]
    return plan


def _tile_call(body, n_in, n_out, args, gather, *, grid, in_specs, out_specs, out_shape, scratch_shapes, name):
    if gather is None:
        outs = pl.pallas_call(body, grid=grid, in_specs=in_specs, out_specs=out_specs, out_shape=out_shape,
                              scratch_shapes=scratch_shapes, compiler_params=_params(("arbitrary",) * len(grid)),
                              name=name)(*args)
        return list(outs), None
    nb = gather.rows_per_step
    n_kv, n_h, hd, n_pos = gather.cache_t.shape[1:]
    kept = gather.sel.shape[0]

    def slab(*idx):
        step = 0
        for size, k in zip(grid, idx):
            step = step * size + k
        step = gather.first_step + step
        return step // n_kv, step % n_kv

    side_in = [gather.cache_t, gather.sel] + ([] if gather.out is None else [gather.out])
    n_side = len(side_in)

    def kern(*refs):
        x_ref, sel_ref = refs[n_in:n_in + 2]
        o_ref = refs[n_in + n_side + n_out]
        _gather_body(x_ref, sel_ref, o_ref, refs[-1])
        body(*refs[:n_in], *refs[n_in + n_side:n_in + n_side + n_out], *refs[n_in + n_side + n_out + 1:-1])

    side_specs = [pl.BlockSpec((nb, 1, n_h, hd, n_pos), lambda *idx: (*slab(*idx), 0, 0, 0)),
                  _const_spec(gather.sel.shape)]
    if gather.out is not None:
        side_specs.append(pl.BlockSpec(memory_space=pl.ANY))
    outs = pl.pallas_call(
        kern,
        grid=grid,
        in_specs=list(in_specs) + side_specs,
        out_specs=list(out_specs) + [pl.BlockSpec((nb, kept, n_h * hd), lambda *idx: (slab(*idx)[0], 0, slab(*idx)[1]))],
        out_shape=list(out_shape) + [jax.ShapeDtypeStruct((gather.cache_t.shape[0], kept, n_kv * n_h * hd), BF16)],
        scratch_shapes=list(scratch_shapes) + [
            pltpu.VMEM((2, n_pos if n_pos <= GATHER_TRANSPOSE_MAX_POSITIONS else 8, LANES), F32)],
        input_output_aliases={} if gather.out is None else {n_in + 2: n_out},
        compiler_params=_params(("arbitrary",) * len(grid)),
        name=name,
    )(*args, *side_in)
    return list(outs[:n_out]), outs[n_out]


def _pool_kernel(h_ref, g_ref, win_ref, wgrp_ref, scale_ref, wout_ref, out_ref, ulast_ref,
                 ext_ref, lvl_ref, *, tm):
    i = pl.program_id(1)
    hist = POOL_HIST

    @pl.when(i == 0)
    def _():
        ext_ref[0:hist, :] = jnp.zeros((hist, D_MODEL), F32)

    x = h_ref[0]
    hn = _rmsnorm(x, g_ref[...]).astype(BF16)
    proj = jnp.dot(hn, win_ref[...], preferred_element_type=F32)
    gate = proj[:, D_MODEL:]
    ext_ref[hist:hist + tm, :] = proj[:, :D_MODEL]

    total = hist + tm
    pos = i * tm + lax.broadcasted_iota(jnp.int32, (tm, 1), 0)
    src = ext_ref
    z_parts = []
    for k, w in enumerate(POOL_WINDOWS):
        shift = w // 2
        lo = 8 * (k + 1)
        c0 = k * POOL_GROUP_WIDTH
        cur = src[lo:total, c0:] + src[lo - shift:total - shift, c0:]
        dst = lvl_ref.at[k % 2]
        dst[lo:total, c0:] = cur
        inv_cnt = 1.0 / jnp.minimum(pos + 1, w).astype(F32)
        wsum = dst[hist:total, c0:c0 + POOL_GROUP_WIDTH]
        r = wsum * inv_cnt - ext_ref[hist:total, c0:c0 + POOL_GROUP_WIDTH]
        z_parts.append(jnp.dot(r.astype(BF16), wgrp_ref[k], preferred_element_type=F32))
        src = dst
    z = jnp.concatenate(z_parts, axis=-1)
    y = z * scale_ref[...] * _silu(gate)
    out_ref[0] = x + jnp.dot(y.astype(BF16), wout_ref[...], preferred_element_type=F32)

    ulast_ref[0] = ext_ref[hist + tm - 16:hist + tm, :]
    ext_ref[0:hist, :] = ext_ref[tm:tm + hist, :]


def _pool_layer(h, g, w_in, w_grp, scale, w_out, *, tm, gather=None):
    b, s, d = h.shape
    kern = functools.partial(_pool_kernel, tm=tm)
    return _tile_call(
        kern, 6, 2, (h, g, w_in, w_grp, scale, w_out), gather,
        grid=(b, s // tm),
        in_specs=[
            pl.BlockSpec((1, tm, d), lambda bi, i: (bi, i, 0)),
            _const_spec((1, d)),
            _const_spec((d, 2 * d)),
            _const_spec((len(POOL_WINDOWS), POOL_GROUP_WIDTH, POOL_GROUP_WIDTH)),
            _const_spec((1, d)),
            _const_spec((d, d)),
        ],
        out_specs=[
            pl.BlockSpec((1, tm, d), lambda bi, i: (bi, i, 0)),
            pl.BlockSpec((1, 16, d), lambda bi, i: (bi, 0, 0)),
        ],
        out_shape=[jax.ShapeDtypeStruct((b, s, d), F32), jax.ShapeDtypeStruct((b, 16, d), F32)],
        scratch_shapes=[pltpu.VMEM((POOL_HIST + tm, d), F32), pltpu.VMEM((2, POOL_HIST + tm, d), F32)],
        name="pool_layer",
    )


def _spool_kernel(h_ref, st_ref, g_ref, win_ref, wgrp_ref, scale_ref, wout_ref, out_ref, nst_ref):
    x = h_ref[...]
    hn = _rmsnorm(x, g_ref[...]).astype(BF16)
    proj = jnp.dot(hn, win_ref[...], preferred_element_type=F32)
    u = proj[:, :D_MODEL]
    gate = proj[:, D_MODEL:]
    z_parts = []
    for k, w in enumerate(POOL_WINDOWS):
        c0 = k * POOL_GROUP_WIDTH
        uk = u[:, c0:c0 + POOL_GROUP_WIDTH]
        tot = uk
        for j in range(1, w):
            tot = tot + st_ref[POOL_STATE - j, :, c0:c0 + POOL_GROUP_WIDTH]
        cnt = float(min(PAST_LEN + 1, w))
        r = tot / cnt - uk
        z_parts.append(jnp.dot(r.astype(BF16), wgrp_ref[k], preferred_element_type=F32))
    z = jnp.concatenate(z_parts, axis=-1)
    y = z * scale_ref[...] * _silu(gate)
    out_ref[...] = x + jnp.dot(y.astype(BF16), wout_ref[...], preferred_element_type=F32)
    nst_ref[0:POOL_STATE - 1] = st_ref[1:POOL_STATE]
    nst_ref[POOL_STATE - 1] = u


def _spool_layer(h, state_t, layer, g, w_in, w_grp, scale, w_out):
    n, d = h.shape
    st_shape = state_t.shape[1:]
    return pl.pallas_call(
        _spool_kernel,
        grid=(1,),
        in_specs=[_const_spec(h.shape),
                  pl.BlockSpec((None, *st_shape), lambda i: (layer, 0, 0, 0), pipeline_mode=pl.Buffered(1)),
                  _const_spec(g.shape), _const_spec(w_in.shape), _const_spec(w_grp.shape), _const_spec(scale.shape),
                  _const_spec(w_out.shape)],
        out_specs=[pl.BlockSpec((n, d), lambda i: (0, 0)), pl.BlockSpec(st_shape, lambda i: (0, 0, 0))],
        out_shape=[jax.ShapeDtypeStruct((n, d), F32), jax.ShapeDtypeStruct(st_shape, F32)],
        compiler_params=_params(("arbitrary",)),
        name="pool_layer_sample",
    )(h, state_t, g, w_in, w_grp, scale, w_out)


def _rope_tables(pos):
    half = ROT_DIM // 2
    inv = 1.0 / (ROPE_THETA ** (jnp.arange(0, ROT_DIM, 2, dtype=jnp.float32) / ROT_DIM))
    ang = pos.astype(jnp.float32)[:, None] * inv[None, :]
    cos, sin = jnp.cos(ang), jnp.sin(ang)
    t = pos.shape[0]
    rest = HEAD_DIM - ROT_DIM
    cos_h = jnp.concatenate([cos, cos, jnp.ones((t, rest), F32)], axis=1)
    sin_a = jnp.concatenate([jnp.zeros((t, half), F32), sin, jnp.zeros((t, rest), F32)], axis=1)
    sin_b = jnp.concatenate([-sin, jnp.zeros((t, half + rest), F32)], axis=1)
    reps = LANES // HEAD_DIM
    return tuple(jnp.tile(a, (1, reps)) for a in (cos_h, sin_a, sin_b))


def _proj_kernel(h_ref, g_ref, w_ref, cos_ref, sa_ref, sb_ref, *rest, tm, chunks, f32_outs, n_bf16):
    bf_refs = rest[:n_bf16]
    f32_refs = rest[n_bf16:n_bf16 + len(f32_outs)]
    y_ref = rest[-1]
    half = ROT_DIM // 2

    hn = _rmsnorm(h_ref[0], g_ref[...]).astype(BF16)
    bi = 0
    n_lt = D_MODEL // LANES
    for c, (rope, scale, dil, bf_out, f32_slot) in enumerate(chunks):
        y = jnp.dot(hn, w_ref[:, c * D_MODEL:(c + 1) * D_MODEL], preferred_element_type=F32)
        for lt in range(n_lt):
            ys = y[:, lt * LANES:(lt + 1) * LANES]
            if rope:
                ys = (ys * cos_ref[...] + pltpu.roll(ys, half, 1) * sa_ref[...]
                      + pltpu.roll(ys, LANES - half, 1) * sb_ref[...])
            y_ref[lt] = ys * scale if scale != 1.0 else ys
        if f32_slot is not None:
            oi, off = f32_slot
            kb = min(tm, f32_outs[oi][0])
            for lt in range(n_lt):
                f32_refs[oi][0, :, off + lt * LANES:off + (lt + 1) * LANES] = y_ref[lt, tm - kb:tm, :]
        if bf_out:
            o_ref = bf_refs[bi]
            bi += 1
            for lt in range(n_lt):
                cols = slice(lt * LANES, (lt + 1) * LANES)
                if dil == 1:
                    o_ref[0, 0, :, cols] = y_ref[lt].astype(BF16)
                else:
                    for r in range(dil):
                        o_ref[0, r, :, cols] = y_ref[lt, pl.ds(r, tm // dil, stride=dil), :].astype(BF16)


def _proj(h, g, w, tables, chunks, f32_outs, *, tm, gather=None):
    b, s, d = h.shape
    n_i = s // tm
    out_shapes, out_specs = [], []
    for rope, scale, dil, bf_out, f32_slot in chunks:
        if bf_out:
            out_shapes.append(jax.ShapeDtypeStruct((b, dil, s // dil, d), BF16))
            out_specs.append(pl.BlockSpec((1, dil, tm // dil, d), lambda bi, i: (bi, 0, i, 0)))
    n_bf16 = len(out_shapes)
    for keep, width in f32_outs:
        kb = min(tm, keep)
        first = n_i - keep // kb
        out_shapes.append(jax.ShapeDtypeStruct((b, keep, width), F32))
        out_specs.append(pl.BlockSpec((1, kb, width), lambda bi, i, first=first: (bi, jnp.maximum(i - first, 0), 0)))
    kern = functools.partial(_proj_kernel, tm=tm, chunks=chunks, f32_outs=f32_outs, n_bf16=n_bf16)
    tab_spec = pl.BlockSpec((tm, LANES), lambda bi, i: (i, 0))
    return _tile_call(
        kern, 6, len(out_shapes), (h, g, w, *tables), gather,
        grid=(b, n_i),
        in_specs=[
            pl.BlockSpec((1, tm, d), lambda bi, i: (bi, i, 0)),
            _const_spec((1, d)),
            _const_spec((d, len(chunks) * D_MODEL)),
            tab_spec, tab_spec, tab_spec,
        ],
        out_specs=out_specs,
        out_shape=out_shapes,
        scratch_shapes=[pltpu.VMEM((d // LANES, tm, LANES), F32)],
        name="proj_rope",
    )


def _attn_kernel(*refs):
    q_refs = refs[0:3]
    k_refs = refs[3:6]
    v_refs = refs[6:9]
    kp_refs = refs[9:12]
    vp_refs = refs[12:15]
    out_ref = refs[15]
    acc_ref, m_ref, l_ref = refs[16:19]
    i = pl.program_id(1)
    blk = SUB_BLOCK

    lane_q = lax.broadcasted_iota(jnp.int32, (blk, LANES), 1) < HEAD_DIM
    key = lax.broadcasted_iota(jnp.int32, (2 * blk, 2 * blk), 1)
    qry = lax.broadcasted_iota(jnp.int32, (2 * blk, 2 * blk), 0) % blk
    band = (key >= qry) & (key <= qry + blk)
    band_first = band & (key >= jnp.where(i > 0, 0, blk))
    ones_v = jnp.ones((2 * blk, LANES), BF16)

    def sub_block(g, dil, r, sub):
        rows_q = slice(sub * blk, (sub + 1) * blk)
        qb = q_refs[g][0, r, rows_q, :]
        if sub == 0:
            kcat = jnp.concatenate([kp_refs[g][0, r], k_refs[g][0, r, 0:blk, :]], axis=0)
            vcat = jnp.concatenate([vp_refs[g][0, r], v_refs[g][0, r, 0:blk, :]], axis=0)
            valid = band_first
        else:
            kcat = k_refs[g][0, r, (sub - 1) * blk:(sub + 1) * blk, :]
            vcat = v_refs[g][0, r, (sub - 1) * blk:(sub + 1) * blk, :]
            valid = band
        zq = jnp.zeros_like(qb)
        q2 = jnp.concatenate([jnp.where(lane_q, qb, zq), jnp.where(lane_q, zq, qb)], axis=0)
        s = lax.dot_general(q2, kcat, (((1,), (1,)), ((), ())), preferred_element_type=F32)
        s = jnp.where(valid, s, -jnp.inf)
        m = jnp.max(s, axis=-1, keepdims=True)
        p = jnp.exp(s - m).astype(BF16)
        pv = jnp.dot(p, jnp.concatenate([vcat, ones_v], axis=1), preferred_element_type=F32)
        acc = jnp.where(lane_q, pv[:blk, :LANES], pv[blk:, :LANES])
        den = jnp.where(lane_q, pv[:blk, LANES:], pv[blk:, LANES:])
        return acc, den, jnp.where(lane_q, m[:blk], m[blk:])

    last = N_GROUPS - 1
    big = ATTN_PATTERNS[last][1]
    assert big * blk == ATTN_TILE and all(dl <= big for _, dl in ATTN_PATTERNS)
    for g, (_, dil) in enumerate(ATTN_PATTERNS[:last]):
        nsub = ATTN_TILE // (dil * blk)

        def residue(r, carry, g=g, dil=dil, nsub=nsub):
            for sub in range(nsub):
                rows = pl.ds(pl.multiple_of(r * (ATTN_TILE // dil) + sub * blk, blk), blk)
                acc_ref[g, rows, :], l_ref[g, rows, :], m_ref[g, rows, :] = sub_block(g, dil, r, sub)
            return carry

        if dil == 1:
            residue(0, 0)
        else:
            lax.fori_loop(0, dil, residue, 0, unroll=min(dil, max(1, 8 // nsub)))

    def widest(rb, carry):
        parts = []
        for g, (_, dil) in enumerate(ATTN_PATTERNS[:last]):
            step = big // dil
            start = (rb % dil) * (ATTN_TILE // dil) + rb // dil
            rows = pl.ds(start, blk, stride=step) if step > 1 else pl.ds(pl.multiple_of(start, blk), blk)
            parts.append((acc_ref[g, rows, :], l_ref[g, rows, :], m_ref[g, rows, :]))
        parts.append(sub_block(last, big, rb, 0))
        mx = parts[0][2]
        for _, _, m in parts[1:]:
            mx = jnp.maximum(mx, m)
        num = jnp.zeros((blk, LANES), F32)
        den = jnp.zeros((blk, LANES), F32)
        for acc, l, m in parts:
            e = jnp.exp(m - mx)
            num = num + e * acc
            den = den + e * l
        out_ref[0, pl.ds(rb, blk, stride=big), :] = num / den
        return carry

    lax.fori_loop(0, big, widest, 0, unroll=8)


def _attention(qs, ks, vs, *, gather=None):
    b = qs[0].shape[0]
    s = qs[0].shape[1] * qs[0].shape[2]
    n_i = s // ATTN_TILE
    n_p = D_MODEL // LANES
    blk = SUB_BLOCK
    cur_specs, prev_specs = [], []
    for _, dil in ATTN_PATTERNS:
        per = ATTN_TILE // dil
        cur_specs.append(pl.BlockSpec((1, dil, per, LANES), lambda bi, i, p: (bi, 0, i, p)))
        nsub = per // blk
        prev_specs.append(pl.BlockSpec((1, dil, blk, LANES),
                                       lambda bi, i, p, nsub=nsub: (bi, 0, jnp.maximum(i * nsub - 1, 0), p)))
    return _tile_call(
        _attn_kernel, 5 * N_GROUPS, 1, (*qs, *ks, *vs, *ks, *vs), gather,
        grid=(b, n_i, n_p),
        in_specs=cur_specs * 3 + prev_specs * 2,
        out_specs=[pl.BlockSpec((1, ATTN_TILE, LANES), lambda bi, i, p: (bi, i, p))],
        out_shape=[jax.ShapeDtypeStruct((b, s, D_MODEL), F32)],
        scratch_shapes=[pltpu.VMEM((N_GROUPS - 1, ATTN_TILE, LANES), F32)] * 3,
        name="dilated_attention",
    )


def _outproj_kernel(h_ref, comb_ref, g_ref, wg_ref, wo_ref, gf_ref, out_ref, *, final):
    x = h_ref[0]
    hn = _rmsnorm(x, g_ref[...]).astype(BF16)
    gate = jnp.dot(hn, wg_ref[...], preferred_element_type=F32)
    y = comb_ref[0] * _silu(gate)
    o = x + jnp.dot(y.astype(BF16), wo_ref[...], preferred_element_type=F32)
    if final:
        o = _rmsnorm(o, gf_ref[...])
    out_ref[0] = o


def _outproj(h, comb, g, w_in, w_out, g_final, *, final, tm, gather=None):
    b, s, d = h.shape
    kern = functools.partial(_outproj_kernel, final=final)
    tile = pl.BlockSpec((1, tm, d), lambda bi, i: (bi, i, 0))
    gate_spec = pl.BlockSpec((d, d), lambda *_: (0, w_in.shape[1] // d - 1), pipeline_mode=pl.Buffered(1))
    return _tile_call(
        kern, 6, 1, (h, comb, g, w_in, w_out, g_final), gather,
        grid=(b, s // tm),
        in_specs=[tile, tile, _const_spec((1, d)), gate_spec, _const_spec((d, d)), _const_spec((1, d))],
        out_specs=[tile],
        out_shape=[jax.ShapeDtypeStruct((b, s, d), F32)],
        scratch_shapes=[],
        name="gate_outproj",
    )


def _sattn_kernel(q_ref, c_ref, seg_ref, segt_ref, acc_ref, m_ref, l_ref, *, bb):
    n = c_ref.shape[1]
    part = 8
    for r0 in range(0, bb, part):
        rows = slice(r0, r0 + part)
        q = q_ref[rows, :].astype(BF16)
        k = c_ref[rows, :, 0:D_MODEL]
        v = c_ref[rows, :, D_MODEL:].astype(F32)
        prod = (k * q[:, None, :]).reshape(part * n, D_MODEL)
        s = jnp.dot(prod, seg_ref[...], preferred_element_type=F32).reshape(part, n, LANES)
        m = jnp.max(s, axis=1)
        p = jnp.exp(s - m[:, None, :])
        l = jnp.sum(p, axis=1)
        pe = jnp.dot(p.reshape(part * n, LANES).astype(BF16), segt_ref[...], preferred_element_type=F32)
        acc_ref[rows, :] = jnp.sum(pe.reshape(part, n, D_MODEL) * v, axis=1)
        m_ref[rows, :] = m
        l_ref[rows, :] = l


def _sample_attention(q, rows, seg, segt, *, bb):
    n_b, n_rows, width = rows.shape
    kern = functools.partial(_sattn_kernel, bb=bb)
    return pl.pallas_call(
        kern,
        grid=(n_b // bb,),
        in_specs=[
            pl.BlockSpec((bb, D_MODEL), lambda i: (i, 0)),
            pl.BlockSpec((bb, n_rows, width), lambda i: (i, 0, 0)),
            _const_spec(seg.shape),
            _const_spec(segt.shape),
        ],
        out_specs=[
            pl.BlockSpec((bb, D_MODEL), lambda i: (i, 0)),
            pl.BlockSpec((bb, LANES), lambda i: (i, 0)),
            pl.BlockSpec((bb, LANES), lambda i: (i, 0)),
        ],
        out_shape=[
            jax.ShapeDtypeStruct((n_b, D_MODEL), F32),
            jax.ShapeDtypeStruct((n_b, LANES), F32),
            jax.ShapeDtypeStruct((n_b, LANES), F32),
        ],
        compiler_params=_params(("arbitrary",)),
        name="sample_attention",
    )(q, rows, seg, segt)


def _expand_heads(x, segt):
    x1 = x.astype(BF16)
    r1 = x - x1.astype(F32)
    x2 = r1.astype(BF16)
    x3 = (r1 - x2.astype(F32)).astype(BF16)
    out = jnp.dot(x1, segt, preferred_element_type=F32)
    out = out + jnp.dot(x2, segt, preferred_element_type=F32)
    return out + jnp.dot(x3, segt, preferred_element_type=F32)


def _soutproj_kernel(h_ref, q_ref, kvn0_ref, kvn1_ref, kvn2_ref, acc_ref, m_ref, l_ref, seg_ref, segt_ref,
                     g_ref, wg_ref, wo_ref, gf_ref, out_ref, *, final):
    kvn_refs = (kvn0_ref, kvn1_ref, kvn2_ref)
    seg, segt = seg_ref[...], segt_ref[...]
    s_new = []
    for g in range(N_GROUPS):
        qg = q_ref[:, g * D_MODEL:(g + 1) * D_MODEL]
        kn = kvn_refs[g][:, 0:D_MODEL]
        s_new.append(jnp.dot((qg * kn).astype(BF16), seg, preferred_element_type=F32))
    ms = [m_ref[g] for g in range(N_GROUPS)]
    mx = ms[0]
    for t in ms[1:] + s_new:
        mx = jnp.maximum(mx, t)
    e_old = [jnp.exp(ms[g] - mx) for g in range(N_GROUPS)]
    e_new = [jnp.exp(s_new[g] - mx) for g in range(N_GROUPS)]
    den = jnp.zeros_like(mx)
    for g in range(N_GROUPS):
        den = den + e_old[g] * l_ref[g] + e_new[g]
    comb = jnp.zeros(h_ref.shape, F32)
    for g in range(N_GROUPS):
        comb = comb + _expand_heads(e_old[g] / den, segt) * acc_ref[g]
        comb = comb + _expand_heads(e_new[g] / den, segt) * kvn_refs[g][:, D_MODEL:]

    x = h_ref[...]
    hn = _rmsnorm(x, g_ref[...]).astype(BF16)
    gate = jnp.dot(hn, wg_ref[...], preferred_element_type=F32)
    y = comb * _silu(gate)
    o = x + jnp.dot(y.astype(BF16), wo_ref[...], preferred_element_type=F32)
    if final:
        o = _rmsnorm(o, gf_ref[...])
    out_ref[...] = o


def _sample_outproj(h, q, kvn, acc, m, l, seg, segt, g, w_gate, w_out, g_final, *, final):
    kern = functools.partial(_soutproj_kernel, final=final)
    return pl.pallas_call(
        kern,
        out_shape=jax.ShapeDtypeStruct(h.shape, F32),
        compiler_params=pltpu.CompilerParams(vmem_limit_bytes=VMEM_LIMIT),
        name="gate_outproj_sample",
    )(h, q, *kvn, acc, m, l, seg, segt, g, w_gate, w_out, g_final)


def kernel(x_prompt, x_sample, state_pool, cache_kv_w128, cache_kv_w512, cache_kv_w2048, g_a, w_a_in, w_a_group,
           a_scale, w_a_out, g_kv, w_kv, g_b, w_b_in, w_b_out, g_final):
    b, s, d = x_prompt.shape
    n_s, t_s, _ = x_sample.shape
    assert d == D_MODEL and t_s == 1 and s % ATTN_TILE == 0
    n_pool = w_a_in.shape[0]
    n_attn = w_b_in.shape[0]
    caches = (cache_kv_w128, cache_kv_w512, cache_kv_w2048)
    dils = tuple(dl for _, dl in ATTN_PATTERNS)
    keeps = tuple(min(w, s) for w, _ in ATTN_PATTERNS)
    for c, dl in zip(caches, dils):
        assert c.shape[1] == SUB_BLOCK * dl
    tm = 512
    tm_kv = 256

    hosts = ([("pool", b * (s // tm))] * n_pool + [("kv", b * (s // tm_kv))]
             + [("q", b * (s // tm)), ("attn", b * (s // ATTN_TILE) * (D_MODEL // LANES)), ("out", b * (s // tm))] * n_attn)
    plan = _plan_gathers(hosts, [c.shape[1] for c in caches], n_s * caches[0].shape[2])
    cache_ts = [jnp.transpose(c, (0, 2, 3, 4, 1)) for c in caches]
    sels, gathered = [], []
    for ci, (c, dl) in enumerate(zip(caches, dils)):
        n_pos, kept = c.shape[1], c.shape[1] // dl
        sels.append((jnp.arange(n_pos, dtype=jnp.int32)[None, :]
                     == dl * jnp.arange(kept, dtype=jnp.int32)[:, None]).astype(BF16))
        n_calls = sum(1 for job in plan if job is not None and job[0] == ci)
        assert n_calls >= 1
        gathered.append(None if n_calls == 1 else jnp.zeros((n_s, kept, c.shape[2] * D_MODEL), BF16))
    plan = iter(plan)

    def hosted(fn, *args, **kwargs):
        job = next(plan)
        if job is None:
            return fn(*args, **kwargs)[0]
        ci, first, nb = job
        outs, gathered[ci] = fn(*args, gather=_Gather(cache_ts[ci], sels[ci], nb, first, gathered[ci]), **kwargs)
        return outs

    wb = lambda a: a.astype(BF16)
    row = lambda a: a.reshape(1, -1)
    w_a_in_b, w_a_group_b, w_a_out_b = wb(w_a_in), wb(w_a_group), wb(w_a_out)
    w_kv_b, w_b_in_b, w_b_out_b = wb(w_kv), wb(w_b_in), wb(w_b_out)
    q_cols = N_GROUPS * D_MODEL

    tab_p = _rope_tables(jnp.arange(s, dtype=jnp.int32))
    tab_s = _rope_tables(jnp.broadcast_to(PAST_LEN + jnp.arange(t_s, dtype=jnp.int32), (n_s,)))
    head_of_lane = jnp.arange(D_MODEL, dtype=jnp.int32) // HEAD_DIM
    seg = (head_of_lane[:, None] == jnp.arange(LANES, dtype=jnp.int32)[None, :]).astype(BF16)
    segt = seg.T

    h_p = x_prompt
    h_s = x_sample.reshape(n_s, d)
    state_t = jnp.transpose(state_pool, (0, 2, 1, 3))
    pool_p, pool_s = [], []
    for li in range(n_pool):
        h_p, ulast = hosted(_pool_layer, h_p, row(g_a[li]), w_a_in_b[li], w_a_group_b[li], row(a_scale[li]),
                            w_a_out_b[li], tm=tm)
        pool_p.append(ulast[:, 16 - POOL_STATE:])
        h_s, nst = _spool_layer(h_s, state_t, li, row(g_a[li]), w_a_in_b[li],
                                w_a_group_b[li], row(a_scale[li]), w_a_out_b[li])
        pool_s.append(jnp.transpose(nst, (1, 0, 2)))

    kv_chunks = []
    for gi in range(N_GROUPS):
        kv_chunks.append((True, 1.0, dils[gi], True, (gi, 0)))
        kv_chunks.append((False, 1.0, dils[gi], True, (gi, D_MODEL)))
    kv_out = hosted(_proj, h_p, row(g_kv), w_kv_b, tab_p, tuple(kv_chunks), tuple((kp, 2 * D_MODEL) for kp in keeps), tm=tm_kv)
    ks =[kv_out[2 * gi] for gi in range(N_GROUPS)]
    vs = [kv_out[2 * gi + 1] for gi in range(N_GROUPS)]
    kv_keep_p = kv_out[2 * N_GROUPS:]

    skv_chunks = []
    for gi in range(N_GROUPS):
        skv_chunks.append((True, 1.0, 1, False, (gi, 0)))
        skv_chunks.append((False, 1.0, 1, False, (gi, D_MODEL)))
    kvn, _ = _proj(h_s[None], row(g_kv), w_kv_b, tab_s, tuple(skv_chunks), ((n_s, 2 * D_MODEL),) * N_GROUPS, tm=n_s)
    kvn = [a[0] for a in kvn]

    q_scale = 1.0 / math.sqrt(HEAD_DIM)
    q_chunks = tuple((True, q_scale, dils[gi], True, None) for gi in range(N_GROUPS))
    sq_chunks = tuple((True, q_scale, 1, False, (0, gi * D_MODEL)) for gi in range(N_GROUPS))
    for lj in range(n_attn):
        final = lj == n_attn - 1
        qs = hosted(_proj, h_p, row(g_b[lj]), w_b_in_b[lj], tab_p, q_chunks, (), tm=tm)
        (comb,) = hosted(_attention, qs, ks, vs)
        (h_p,) = hosted(_outproj, h_p, comb, row(g_b[lj]), w_b_in_b[lj], w_b_out_b[lj], row(g_final), final=final, tm=tm)
    assert next(plan, None) is None

    for lj in range(n_attn):
        final = lj == n_attn - 1
        w_gate = w_b_in_b[lj][:, q_cols:]
        (q_s,), _ = _proj(h_s[None], row(g_b[lj]), w_b_in_b[lj], tab_s, sq_chunks, ((n_s, q_cols),), tm=n_s)
        q_s = q_s[0]
        parts = [_sample_attention(q_s[:, gi * D_MODEL:(gi + 1) * D_MODEL], gathered[gi], seg, segt, bb=min(n_s, 16))
                 for gi in range(N_GROUPS)]
        acc = jnp.stack([pt[0] for pt in parts])
        m = jnp.stack([pt[1] for pt in parts])
        l = jnp.stack([pt[2] for pt in parts])
        h_s = _sample_outproj(h_s, q_s, kvn, acc, m, l, seg, segt, row(g_b[lj]), w_gate, w_b_out_b[lj],
                              row(g_final), final=final)

    kv_shape = (2, N_HEADS, HEAD_DIM)
    outs = [h_p, h_s.reshape(n_s, t_s, d), jnp.stack(pool_p), jnp.stack(pool_s)]
    for gi in range(N_GROUPS):
        outs.append(kv_keep_p[gi].reshape(b, keeps[gi], *kv_shape))
        outs.append(kvn[gi].reshape(n_s, t_s, *kv_shape))
    return tuple(outs)
```

```python
import functools
import math
from typing import NamedTuple

import jax
import jax.numpy as jnp
from jax import lax
from jax.experimental import pallas as pl
from jax.experimental.pallas import tpu as pltpu

D_MODEL = 1024
HEAD_DIM = 64
N_HEADS = D_MODEL // HEAD_DIM
ROT_DIM = HEAD_DIM // 4
ROPE_THETA = 500000.0
POOL_WINDOWS = (2, 4, 8, 16)
POOL_GROUP_WIDTH = D_MODEL // len(POOL_WINDOWS)
POOL_STATE = max(POOL_WINDOWS) - 1
ATTN_PATTERNS = ((128, 1), (512, 4), (2048, 16))
N_GROUPS = len(ATTN_PATTERNS)
SUB_BLOCK = 128
RMS_EPS = 1e-6
PAST_LEN = 2048

LANES = 128
ATTN_TILE = SUB_BLOCK * max(d for _, d in ATTN_PATTERNS)
POOL_HIST = 32
VMEM_LIMIT = 56 * 1024 * 1024

F32 = jnp.float32
BF16 = jnp.bfloat16


def _rmsnorm(x, g):
    ms = jnp.mean(x * x, axis=-1, keepdims=True)
    return x * lax.rsqrt(ms + RMS_EPS) * g


def _silu(x):
    return x * (1.0 / (1.0 + jnp.exp(-x)))


def _const_spec(shape):
    return pl.BlockSpec(shape, lambda *_: (0,) * len(shape), pipeline_mode=pl.Buffered(1))


def _params(sem):
    return pltpu.CompilerParams(dimension_semantics=sem, vmem_limit_bytes=VMEM_LIMIT)


GATHER_FEATURES = 256


class _Gather(NamedTuple):
    cache_t: jax.Array
    sel: jax.Array
    rows_per_step: int
    first_step: int
    out: jax.Array | None


GATHER_TRANSPOSE_MAX_POSITIONS = 512


def _gather_body(x_ref, sel_ref, o_ref, t_ref):
    nb, _, _, hd, n_pos = x_ref.shape
    kept = o_ref.shape[1]
    if n_pos > GATHER_TRANSPOSE_MAX_POSITIONS:
        heads = GATHER_FEATURES // hd
        sel = sel_ref[...]
        for bi in range(nb):
            for c in range(N_HEADS // heads):
                x = x_ref[bi, 0, heads * c:heads * (c + 1)].reshape(GATHER_FEATURES, n_pos).astype(BF16)
                y = lax.dot_general(sel, x, (((1,), (1,)), ((), ())), preferred_element_type=F32)
                o_ref[bi, :, c * GATHER_FEATURES:(c + 1) * GATHER_FEATURES] = y.astype(BF16)
        return
    dil = n_pos // kept
    pair = LANES // hd
    slot = 0
    for bi in range(nb):
        for c in range(N_HEADS // pair):
            x = x_ref[bi, 0, pair * c:pair * (c + 1)].reshape(LANES, n_pos)
            t_ref[slot] = x.T
            rows = t_ref[slot, pl.ds(0, kept, stride=dil), :] if dil > 1 else t_ref[slot]
            o_ref[bi, :, c * LANES:(c + 1) * LANES] = rows.astype(BF16)
            slot = 1 - slot


GATHER_SLAB_POSITIONS = 2048
GATHER_HOST_PREFERENCE = ("attn", "pool", "q", "kv", "out")


def _plan_gathers(hosts, cache_positions, halves):
    plan = [None] * len(hosts)
    by_preference = sorted(range(len(hosts)), key=lambda k: GATHER_HOST_PREFERENCE.index(hosts[k][0]))
    for ci in sorted(range(len(cache_positions)), key=lambda c: -cache_positions[c]):
        nb = max(1, GATHER_SLAB_POSITIONS // cache_positions[ci])
        chosen = None
        while nb >= 1 and chosen is None:
            want, picked = halves // nb, []
            for k in by_preference:
                if plan[k] is None and hosts[k][1] <= want:
                    picked.append(k)
                    want -= hosts[k][1]
            if want == 0 and halves % nb == 0:
                chosen = sorted(picked)
            else:
                nb //= 2
        assert chosen is not None, "no set of hosting calls covers this cache"
        first = 0
        for k in chosen:
            plan[k] = (ci, first, nb)
            first += hosts[k][1]
    return plan


def _tile_call(body, n_in, n_out, args, gather, *, grid, in_specs, out_specs, out_shape, scratch_shapes, name):
    if gather is None:
        outs = pl.pallas_call(body, grid=grid, in_specs=in_specs, out_specs=out_specs, out_shape=out_shape,
                              scratch_shapes=scratch_shapes, compiler_params=_params(("arbitrary",) * len(grid)),
                              name=name)(*args)
        return list(outs), None
    nb = gather.rows_per_step
    n_kv, n_h, hd, n_pos = gather.cache_t.shape[1:]
    kept = gather.sel.shape[0]

    def slab(*idx):
        step = 0
        for size, k in zip(grid, idx):
            step = step * size + k
        step = gather.first_step + step
        return step // n_kv, step % n_kv

    side_in = [gather.cache_t, gather.sel] + ([] if gather.out is None else [gather.out])
    n_side = len(side_in)

    def kern(*refs):
        x_ref, sel_ref = refs[n_in:n_in + 2]
        o_ref = refs[n_in + n_side + n_out]
        _gather_body(x_ref, sel_ref, o_ref, refs[-1])
        body(*refs[:n_in], *refs[n_in + n_side:n_in + n_side + n_out], *refs[n_in + n_side + n_out + 1:-1])

    side_specs = [pl.BlockSpec((nb, 1, n_h, hd, n_pos), lambda *idx: (*slab(*idx), 0, 0, 0)),
                  _const_spec(gather.sel.shape)]
    if gather.out is not None:
        side_specs.append(pl.BlockSpec(memory_space=pl.ANY))
    outs = pl.pallas_call(
        kern,
        grid=grid,
        in_specs=list(in_specs) + side_specs,
        out_specs=list(out_specs) + [pl.BlockSpec((nb, kept, n_h * hd), lambda *idx: (slab(*idx)[0], 0, slab(*idx)[1]))],
        out_shape=list(out_shape) + [jax.ShapeDtypeStruct((gather.cache_t.shape[0], kept, n_kv * n_h * hd), BF16)],
        scratch_shapes=list(scratch_shapes) + [
            pltpu.VMEM((2, n_pos if n_pos <= GATHER_TRANSPOSE_MAX_POSITIONS else 8, LANES), F32)],
        input_output_aliases={} if gather.out is None else {n_in + 2: n_out},
        compiler_params=_params(("arbitrary",) * len(grid)),
        name=name,
    )(*args, *side_in)
    return list(outs[:n_out]), outs[n_out]


def _pool_kernel(h_ref, g_ref, win_ref, wgrp_ref, scale_ref, wout_ref, out_ref, ulast_ref,
                 ext_ref, lvl_ref, *, tm):
    i = pl.program_id(1)
    hist = POOL_HIST

    @pl.when(i == 0)
    def _():
        ext_ref[0:hist, :] = jnp.zeros((hist, D_MODEL), F32)

    x = h_ref[0]
    hn = _rmsnorm(x, g_ref[...]).astype(BF16)
    proj = jnp.dot(hn, win_ref[...], preferred_element_type=F32)
    gate = proj[:, D_MODEL:]
    ext_ref[hist:hist + tm, :] = proj[:, :D_MODEL]

    total = hist + tm
    pos = i * tm + lax.broadcasted_iota(jnp.int32, (tm, 1), 0)
    src = ext_ref
    z_parts = []
    for k, w in enumerate(POOL_WINDOWS):
        shift = w // 2
        lo = 8 * (k + 1)
        c0 = k * POOL_GROUP_WIDTH
        cur = src[lo:total, c0:] + src[lo - shift:total - shift, c0:]
        dst = lvl_ref.at[k % 2]
        dst[lo:total, c0:] = cur
        inv_cnt = 1.0 / jnp.minimum(pos + 1, w).astype(F32)
        wsum = dst[hist:total, c0:c0 + POOL_GROUP_WIDTH]
        r = wsum * inv_cnt - ext_ref[hist:total, c0:c0 + POOL_GROUP_WIDTH]
        z_parts.append(jnp.dot(r.astype(BF16), wgrp_ref[k], preferred_element_type=F32))
        src = dst
    z = jnp.concatenate(z_parts, axis=-1)
    y = z * scale_ref[...] * _silu(gate)
    out_ref[0] = x + jnp.dot(y.astype(BF16), wout_ref[...], preferred_element_type=F32)

    ulast_ref[0] = ext_ref[hist + tm - 16:hist + tm, :]
    ext_ref[0:hist, :] = ext_ref[tm:tm + hist, :]


def _pool_layer(h, g, w_in, w_grp, scale, w_out, *, tm, gather=None):
    b, s, d = h.shape
    kern = functools.partial(_pool_kernel, tm=tm)
    return _tile_call(
        kern, 6, 2, (h, g, w_in, w_grp, scale, w_out), gather,
        grid=(b, s // tm),
        in_specs=[
            pl.BlockSpec((1, tm, d), lambda bi, i: (bi, i, 0)),
            _const_spec((1, d)),
            _const_spec((d, 2 * d)),
            _const_spec((len(POOL_WINDOWS), POOL_GROUP_WIDTH, POOL_GROUP_WIDTH)),
            _const_spec((1, d)),
            _const_spec((d, d)),
        ],
        out_specs=[
            pl.BlockSpec((1, tm, d), lambda bi, i: (bi, i, 0)),
            pl.BlockSpec((1, 16, d), lambda bi, i: (bi, 0, 0)),
        ],
        out_shape=[jax.ShapeDtypeStruct((b, s, d), F32), jax.ShapeDtypeStruct((b, 16, d), F32)],
        scratch_shapes=[pltpu.VMEM((POOL_HIST + tm, d), F32), pltpu.VMEM((2, POOL_HIST + tm, d), F32)],
        name="pool_layer",
    )


def _spool_kernel(h_ref, st_ref, g_ref, win_ref, wgrp_ref, scale_ref, wout_ref, out_ref, nst_ref):
    x = h_ref[...]
    hn = _rmsnorm(x, g_ref[...]).astype(BF16)
    proj = jnp.dot(hn, win_ref[...], preferred_element_type=F32)
    u = proj[:, :D_MODEL]
    gate = proj[:, D_MODEL:]
    z_parts = []
    for k, w in enumerate(POOL_WINDOWS):
        c0 = k * POOL_GROUP_WIDTH
        uk = u[:, c0:c0 + POOL_GROUP_WIDTH]
        tot = uk
        for j in range(1, w):
            tot = tot + st_ref[POOL_STATE - j, :, c0:c0 + POOL_GROUP_WIDTH]
        cnt = float(min(PAST_LEN + 1, w))
        r = tot / cnt - uk
        z_parts.append(jnp.dot(r.astype(BF16), wgrp_ref[k], preferred_element_type=F32))
    z = jnp.concatenate(z_parts, axis=-1)
    y = z * scale_ref[...] * _silu(gate)
    out_ref[...] = x + jnp.dot(y.astype(BF16), wout_ref[...], preferred_element_type=F32)
    nst_ref[0:POOL_STATE - 1] = st_ref[1:POOL_STATE]
    nst_ref[POOL_STATE - 1] = u


def _spool_layer(h, state_t, layer, g, w_in, w_grp, scale, w_out):
    n, d = h.shape
    st_shape = state_t.shape[1:]
    return pl.pallas_call(
        _spool_kernel,
        grid=(1,),
        in_specs=[_const_spec(h.shape),
                  pl.BlockSpec((None, *st_shape), lambda i: (layer, 0, 0, 0), pipeline_mode=pl.Buffered(1)),
                  _const_spec(g.shape), _const_spec(w_in.shape), _const_spec(w_grp.shape), _const_spec(scale.shape),
                  _const_spec(w_out.shape)],
        out_specs=[pl.BlockSpec((n, d), lambda i: (0, 0)), pl.BlockSpec(st_shape, lambda i: (0, 0, 0))],
        out_shape=[jax.ShapeDtypeStruct((n, d), F32), jax.ShapeDtypeStruct(st_shape, F32)],
        compiler_params=_params(("arbitrary",)),
        name="pool_layer_sample",
    )(h, state_t, g, w_in, w_grp, scale, w_out)


def _rope_tables(pos):
    half = ROT_DIM // 2
    inv = 1.0 / (ROPE_THETA ** (jnp.arange(0, ROT_DIM, 2, dtype=jnp.float32) / ROT_DIM))
    ang = pos.astype(jnp.float32)[:, None] * inv[None, :]
    cos, sin = jnp.cos(ang), jnp.sin(ang)
    t = pos.shape[0]
    rest = HEAD_DIM - ROT_DIM
    cos_h = jnp.concatenate([cos, cos, jnp.ones((t, rest), F32)], axis=1)
    sin_a = jnp.concatenate([jnp.zeros((t, half), F32), sin, jnp.zeros((t, rest), F32)], axis=1)
    sin_b = jnp.concatenate([-sin, jnp.zeros((t, half + rest), F32)], axis=1)
    reps = LANES // HEAD_DIM
    return tuple(jnp.tile(a, (1, reps)) for a in (cos_h, sin_a, sin_b))


def _proj_kernel(h_ref, g_ref, w_ref, cos_ref, sa_ref, sb_ref, *rest, tm, chunks, f32_outs, n_bf16):
    bf_refs = rest[:n_bf16]
    f32_refs = rest[n_bf16:n_bf16 + len(f32_outs)]
    y_ref = rest[-1]
    half = ROT_DIM // 2

    hn = _rmsnorm(h_ref[0], g_ref[...]).astype(BF16)
    bi = 0
    n_lt = D_MODEL // LANES
    for c, (rope, scale, dil, bf_out, f32_slot) in enumerate(chunks):
        y = jnp.dot(hn, w_ref[:, c * D_MODEL:(c + 1) * D_MODEL], preferred_element_type=F32)
        for lt in range(n_lt):
            ys = y[:, lt * LANES:(lt + 1) * LANES]
            if rope:
                ys = (ys * cos_ref[...] + pltpu.roll(ys, half, 1) * sa_ref[...]
                      + pltpu.roll(ys, LANES - half, 1) * sb_ref[...])
            y_ref[lt] = ys * scale if scale != 1.0 else ys
        if f32_slot is not None:
            oi, off = f32_slot
            kb = min(tm, f32_outs[oi][0])
            for lt in range(n_lt):
                f32_refs[oi][0, :, off + lt * LANES:off + (lt + 1) * LANES] = y_ref[lt, tm - kb:tm, :]
        if bf_out:
            o_ref = bf_refs[bi]
            bi += 1
            for lt in range(n_lt):
                cols = slice(lt * LANES, (lt + 1) * LANES)
                if dil == 1:
                    o_ref[0, 0, :, cols] = y_ref[lt].astype(BF16)
                else:
                    for r in range(dil):
                        o_ref[0, r, :, cols] = y_ref[lt, pl.ds(r, tm // dil, stride=dil), :].astype(BF16)


def _proj(h, g, w, tables, chunks, f32_outs, *, tm, gather=None):
    b, s, d = h.shape
    n_i = s // tm
    out_shapes, out_specs = [], []
    for rope, scale, dil, bf_out, f32_slot in chunks:
        if bf_out:
            out_shapes.append(jax.ShapeDtypeStruct((b, dil, s // dil, d), BF16))
            out_specs.append(pl.BlockSpec((1, dil, tm // dil, d), lambda bi, i: (bi, 0, i, 0)))
    n_bf16 = len(out_shapes)
    for keep, width in f32_outs:
        kb = min(tm, keep)
        first = n_i - keep // kb
        out_shapes.append(jax.ShapeDtypeStruct((b, keep, width), F32))
        out_specs.append(pl.BlockSpec((1, kb, width), lambda bi, i, first=first: (bi, jnp.maximum(i - first, 0), 0)))
    kern = functools.partial(_proj_kernel, tm=tm, chunks=chunks, f32_outs=f32_outs, n_bf16=n_bf16)
    tab_spec = pl.BlockSpec((tm, LANES), lambda bi, i: (i, 0))
    return _tile_call(
        kern, 6, len(out_shapes), (h, g, w, *tables), gather,
        grid=(b, n_i),
        in_specs=[
            pl.BlockSpec((1, tm, d), lambda bi, i: (bi, i, 0)),
            _const_spec((1, d)),
            _const_spec((d, len(chunks) * D_MODEL)),
            tab_spec, tab_spec, tab_spec,
        ],
        out_specs=out_specs,
        out_shape=out_shapes,
        scratch_shapes=[pltpu.VMEM((d // LANES, tm, LANES), F32)],
        name="proj_rope",
    )


def _attn_kernel(*refs):
    q_refs = refs[0:3]
    k_refs = refs[3:6]
    v_refs = refs[6:9]
    kp_refs = refs[9:12]
    vp_refs = refs[12:15]
    out_ref = refs[15]
    acc_ref, m_ref, l_ref = refs[16:19]
    i = pl.program_id(1)
    blk = SUB_BLOCK

    lane_q = lax.broadcasted_iota(jnp.int32, (blk, LANES), 1) < HEAD_DIM
    key = lax.broadcasted_iota(jnp.int32, (2 * blk, 2 * blk), 1)
    qry = lax.broadcasted_iota(jnp.int32, (2 * blk, 2 * blk), 0) % blk
    band = (key >= qry) & (key <= qry + blk)
    band_first = band & (key >= jnp.where(i > 0, 0, blk))
    ones_v = jnp.ones((2 * blk, LANES), BF16)

    def sub_block(g, dil, r, sub):
        rows_q = slice(sub * blk, (sub + 1) * blk)
        qb = q_refs[g][0, r, rows_q, :]
        if sub == 0:
            kcat = jnp.concatenate([kp_refs[g][0, r], k_refs[g][0, r, 0:blk, :]], axis=0)
            vcat = jnp.concatenate([vp_refs[g][0, r], v_refs[g][0, r, 0:blk, :]], axis=0)
            valid = band_first
        else:
            kcat = k_refs[g][0, r, (sub - 1) * blk:(sub + 1) * blk, :]
            vcat = v_refs[g][0, r, (sub - 1) * blk:(sub + 1) * blk, :]
            valid = band
        zq = jnp.zeros_like(qb)
        q2 = jnp.concatenate([jnp.where(lane_q, qb, zq), jnp.where(lane_q, zq, qb)], axis=0)
        s = lax.dot_general(q2, kcat, (((1,), (1,)), ((), ())), preferred_element_type=F32)
        s = jnp.where(valid, s, -jnp.inf)
        m = jnp.max(s, axis=-1, keepdims=True)
        p = jnp.exp(s - m).astype(BF16)
        pv = jnp.dot(p, jnp.concatenate([vcat, ones_v], axis=1), preferred_element_type=F32)
        acc = jnp.where(lane_q, pv[:blk, :LANES], pv[blk:, :LANES])
        den = jnp.where(lane_q, pv[:blk, LANES:], pv[blk:, LANES:])
        return acc, den, jnp.where(lane_q, m[:blk], m[blk:])

    last = N_GROUPS - 1
    big = ATTN_PATTERNS[last][1]
    assert big * blk == ATTN_TILE and all(dl <= big for _, dl in ATTN_PATTERNS)
    for g, (_, dil) in enumerate(ATTN_PATTERNS[:last]):
        nsub = ATTN_TILE // (dil * blk)

        def residue(r, carry, g=g, dil=dil, nsub=nsub):
            for sub in range(nsub):
                rows = pl.ds(pl.multiple_of(r * (ATTN_TILE // dil) + sub * blk, blk), blk)
                acc_ref[g, rows, :], l_ref[g, rows, :], m_ref[g, rows, :] = sub_block(g, dil, r, sub)
            return carry

        if dil == 1:
            residue(0, 0)
        else:
            lax.fori_loop(0, dil, residue, 0, unroll=min(dil, max(1, 16 // nsub)))

    def widest(rb, carry):
        parts = []
        for g, (_, dil) in enumerate(ATTN_PATTERNS[:last]):
            step = big // dil
            start = (rb % dil) * (ATTN_TILE // dil) + rb // dil
            rows = pl.ds(start, blk, stride=step) if step > 1 else pl.ds(pl.multiple_of(start, blk), blk)
            parts.append((acc_ref[g, rows, :], l_ref[g, rows, :], m_ref[g, rows, :]))
        parts.append(sub_block(last, big, rb, 0))
        mx = parts[0][2]
        for _, _, m in parts[1:]:
            mx = jnp.maximum(mx, m)
        num = jnp.zeros((blk, LANES), F32)
        den = jnp.zeros((blk, LANES), F32)
        for acc, l, m in parts:
            e = jnp.exp(m - mx)
            num = num + e * acc
            den = den + e * l
        out_ref[0, pl.ds(rb, blk, stride=big), :] = num / den
        return carry

    lax.fori_loop(0, big, widest, 0, unroll=16)


def _attention(qs, ks, vs, *, gather=None):
    b = qs[0].shape[0]
    s = qs[0].shape[1] * qs[0].shape[2]
    n_i = s // ATTN_TILE
    n_p = D_MODEL // LANES
    blk = SUB_BLOCK
    cur_specs, prev_specs = [], []
    for _, dil in ATTN_PATTERNS:
        per = ATTN_TILE // dil
        cur_specs.append(pl.BlockSpec((1, dil, per, LANES), lambda bi, i, p: (bi, 0, i, p)))
        nsub = per // blk
        prev_specs.append(pl.BlockSpec((1, dil, blk, LANES),
                                       lambda bi, i, p, nsub=nsub: (bi, 0, jnp.maximum(i * nsub - 1, 0), p)))
    return _tile_call(
        _attn_kernel, 5 * N_GROUPS, 1, (*qs, *ks, *vs, *ks, *vs), gather,
        grid=(b, n_i, n_p),
        in_specs=cur_specs * 3 + prev_specs * 2,
        out_specs=[pl.BlockSpec((1, ATTN_TILE, LANES), lambda bi, i, p: (bi, i, p))],
        out_shape=[jax.ShapeDtypeStruct((b, s, D_MODEL), F32)],
        scratch_shapes=[pltpu.VMEM((N_GROUPS - 1, ATTN_TILE, LANES), F32)] * 3,
        name="dilated_attention",
    )


def _outproj_kernel(h_ref, comb_ref, g_ref, wg_ref, wo_ref, gf_ref, out_ref, *, final):
    x = h_ref[0]
    hn = _rmsnorm(x, g_ref[...]).astype(BF16)
    gate = jnp.dot(hn, wg_ref[...], preferred_element_type=F32)
    y = comb_ref[0] * _silu(gate)
    o = x + jnp.dot(y.astype(BF16), wo_ref[...], preferred_element_type=F32)
    if final:
        o = _rmsnorm(o, gf_ref[...])
    out_ref[0] = o


def _outproj(h, comb, g, w_in, w_out, g_final, *, final, tm, gather=None):
    b, s, d = h.shape
    kern = functools.partial(_outproj_kernel, final=final)
    tile = pl.BlockSpec((1, tm, d), lambda bi, i: (bi, i, 0))
    gate_spec = pl.BlockSpec((d, d), lambda *_: (0, w_in.shape[1] // d - 1), pipeline_mode=pl.Buffered(1))
    return _tile_call(
        kern, 6, 1, (h, comb, g, w_in, w_out, g_final), gather,
        grid=(b, s // tm),
        in_specs=[tile, tile, _const_spec((1, d)), gate_spec, _const_spec((d, d)), _const_spec((1, d))],
        out_specs=[tile],
        out_shape=[jax.ShapeDtypeStruct((b, s, d), F32)],
        scratch_shapes=[],
        name="gate_outproj",
    )


def _sattn_kernel(q_ref, c_ref, seg_ref, segt_ref, acc_ref, m_ref, l_ref, *, bb):
    n = c_ref.shape[1]
    part = 8
    for r0 in range(0, bb, part):
        rows = slice(r0, r0 + part)
        q = q_ref[rows, :].astype(BF16)
        k = c_ref[rows, :, 0:D_MODEL]
        v = c_ref[rows, :, D_MODEL:].astype(F32)
        prod = (k * q[:, None, :]).reshape(part * n, D_MODEL)
        s = jnp.dot(prod, seg_ref[...], preferred_element_type=F32).reshape(part, n, LANES)
        m = jnp.max(s, axis=1)
        p = jnp.exp(s - m[:, None, :])
        l = jnp.sum(p, axis=1)
        pe = jnp.dot(p.reshape(part * n, LANES).astype(BF16), segt_ref[...], preferred_element_type=F32)
        acc_ref[rows, :] = jnp.sum(pe.reshape(part, n, D_MODEL) * v, axis=1)
        m_ref[rows, :] = m
        l_ref[rows, :] = l


def _sample_attention(q, rows, seg, segt, *, bb):
    n_b, n_rows, width = rows.shape
    kern = functools.partial(_sattn_kernel, bb=bb)
    return pl.pallas_call(
        kern,
        grid=(n_b // bb,),
        in_specs=[
            pl.BlockSpec((bb, D_MODEL), lambda i: (i, 0)),
            pl.BlockSpec((bb, n_rows, width), lambda i: (i, 0, 0)),
            _const_spec(seg.shape),
            _const_spec(segt.shape),
        ],
        out_specs=[
            pl.BlockSpec((bb, D_MODEL), lambda i: (i, 0)),
            pl.BlockSpec((bb, LANES), lambda i: (i, 0)),
            pl.BlockSpec((bb, LANES), lambda i: (i, 0)),
        ],
        out_shape=[
            jax.ShapeDtypeStruct((n_b, D_MODEL), F32),
            jax.ShapeDtypeStruct((n_b, LANES), F32),
            jax.ShapeDtypeStruct((n_b, LANES), F32),
        ],
        compiler_params=_params(("arbitrary",)),
        name="sample_attention",
    )(q, rows, seg, segt)


def _expand_heads(x, segt):
    x1 = x.astype(BF16)
    r1 = x - x1.astype(F32)
    x2 = r1.astype(BF16)
    x3 = (r1 - x2.astype(F32)).astype(BF16)
    out = jnp.dot(x1, segt, preferred_element_type=F32)
    out = out + jnp.dot(x2, segt, preferred_element_type=F32)
    return out + jnp.dot(x3, segt, preferred_element_type=F32)


def _soutproj_kernel(h_ref, q_ref, kvn0_ref, kvn1_ref, kvn2_ref, acc_ref, m_ref, l_ref, seg_ref, segt_ref,
                     g_ref, wg_ref, wo_ref, gf_ref, out_ref, *, final):
    kvn_refs = (kvn0_ref, kvn1_ref, kvn2_ref)
    seg, segt = seg_ref[...], segt_ref[...]
    s_new = []
    for g in range(N_GROUPS):
        qg = q_ref[:, g * D_MODEL:(g + 1) * D_MODEL]
        kn = kvn_refs[g][:, 0:D_MODEL]
        s_new.append(jnp.dot((qg * kn).astype(BF16), seg, preferred_element_type=F32))
    ms = [m_ref[g] for g in range(N_GROUPS)]
    mx = ms[0]
    for t in ms[1:] + s_new:
        mx = jnp.maximum(mx, t)
    e_old = [jnp.exp(ms[g] - mx) for g in range(N_GROUPS)]
    e_new = [jnp.exp(s_new[g] - mx) for g in range(N_GROUPS)]
    den = jnp.zeros_like(mx)
    for g in range(N_GROUPS):
        den = den + e_old[g] * l_ref[g] + e_new[g]
    comb = jnp.zeros(h_ref.shape, F32)
    for g in range(N_GROUPS):
        comb = comb + _expand_heads(e_old[g] / den, segt) * acc_ref[g]
        comb = comb + _expand_heads(e_new[g] / den, segt) * kvn_refs[g][:, D_MODEL:]

    x = h_ref[...]
    hn = _rmsnorm(x, g_ref[...]).astype(BF16)
    gate = jnp.dot(hn, wg_ref[...], preferred_element_type=F32)
    y = comb * _silu(gate)
    o = x + jnp.dot(y.astype(BF16), wo_ref[...], preferred_element_type=F32)
    if final:
        o = _rmsnorm(o, gf_ref[...])
    out_ref[...] = o


def _sample_outproj(h, q, kvn, acc, m, l, seg, segt, g, w_gate, w_out, g_final, *, final):
    kern = functools.partial(_soutproj_kernel, final=final)
    return pl.pallas_call(
        kern,
        out_shape=jax.ShapeDtypeStruct(h.shape, F32),
        compiler_params=pltpu.CompilerParams(vmem_limit_bytes=VMEM_LIMIT),
        name="gate_outproj_sample",
    )(h, q, *kvn, acc, m, l, seg, segt, g, w_gate, w_out, g_final)


def kernel(x_prompt, x_sample, state_pool, cache_kv_w128, cache_kv_w512, cache_kv_w2048, g_a, w_a_in, w_a_group,
           a_scale, w_a_out, g_kv, w_kv, g_b, w_b_in, w_b_out, g_final):
    b, s, d = x_prompt.shape
    n_s, t_s, _ = x_sample.shape
    assert d == D_MODEL and t_s == 1 and s % ATTN_TILE == 0
    n_pool = w_a_in.shape[0]
    n_attn = w_b_in.shape[0]
    caches = (cache_kv_w128, cache_kv_w512, cache_kv_w2048)
    dils = tuple(dl for _, dl in ATTN_PATTERNS)
    keeps = tuple(min(w, s) for w, _ in ATTN_PATTERNS)
    for c, dl in zip(caches, dils):
        assert c.shape[1] == SUB_BLOCK * dl
    tm = 512
    tm_kv = 256

    hosts = ([("pool", b * (s // tm))] * n_pool + [("kv", b * (s // tm_kv))]
             + [("q", b * (s // tm)), ("attn", b * (s // ATTN_TILE) * (D_MODEL // LANES)), ("out", b * (s // tm))] * n_attn)
    plan = _plan_gathers(hosts, [c.shape[1] for c in caches], n_s * caches[0].shape[2])
    cache_ts = [jnp.transpose(c, (0, 2, 3, 4, 1)) for c in caches]
    sels, gathered = [], []
    for ci, (c, dl) in enumerate(zip(caches, dils)):
        n_pos, kept = c.shape[1], c.shape[1] // dl
        sels.append((jnp.arange(n_pos, dtype=jnp.int32)[None, :]
                     == dl * jnp.arange(kept, dtype=jnp.int32)[:, None]).astype(BF16))
        n_calls = sum(1 for job in plan if job is not None and job[0] == ci)
        assert n_calls >= 1
        gathered.append(None if n_calls == 1 else jnp.zeros((n_s, kept, c.shape[2] * D_MODEL), BF16))
    plan = iter(plan)

    def hosted(fn, *args, **kwargs):
        job = next(plan)
        if job is None:
            return fn(*args, **kwargs)[0]
        ci, first, nb = job
        outs, gathered[ci] = fn(*args, gather=_Gather(cache_ts[ci], sels[ci], nb, first, gathered[ci]), **kwargs)
        return outs

    wb = lambda a: a.astype(BF16)
    row = lambda a: a.reshape(1, -1)
    w_a_in_b, w_a_group_b, w_a_out_b = wb(w_a_in), wb(w_a_group), wb(w_a_out)
    w_kv_b, w_b_in_b, w_b_out_b = wb(w_kv), wb(w_b_in), wb(w_b_out)
    q_cols = N_GROUPS * D_MODEL

    tab_p = _rope_tables(jnp.arange(s, dtype=jnp.int32))
    tab_s = _rope_tables(jnp.broadcast_to(PAST_LEN + jnp.arange(t_s, dtype=jnp.int32), (n_s,)))
    head_of_lane = jnp.arange(D_MODEL, dtype=jnp.int32) // HEAD_DIM
    seg = (head_of_lane[:, None] == jnp.arange(LANES, dtype=jnp.int32)[None, :]).astype(BF16)
    segt = seg.T

    h_p = x_prompt
    h_s = x_sample.reshape(n_s, d)
    state_t = jnp.transpose(state_pool, (0, 2, 1, 3))
    pool_p, pool_s = [], []
    for li in range(n_pool):
        h_p, ulast = hosted(_pool_layer, h_p, row(g_a[li]), w_a_in_b[li], w_a_group_b[li], row(a_scale[li]),
                            w_a_out_b[li], tm=tm)
        pool_p.append(ulast[:, 16 - POOL_STATE:])
        h_s, nst = _spool_layer(h_s, state_t, li, row(g_a[li]), w_a_in_b[li],
                                w_a_group_b[li], row(a_scale[li]), w_a_out_b[li])
        pool_s.append(jnp.transpose(nst, (1, 0, 2)))

    kv_chunks = []
    for gi in range(N_GROUPS):
        kv_chunks.append((True, 1.0, dils[gi], True, (gi, 0)))
        kv_chunks.append((False, 1.0, dils[gi], True, (gi, D_MODEL)))
    kv_out = hosted(_proj, h_p, row(g_kv), w_kv_b, tab_p, tuple(kv_chunks), tuple((kp, 2 * D_MODEL) for kp in keeps), tm=tm_kv)
    ks =[kv_out[2 * gi] for gi in range(N_GROUPS)]
    vs = [kv_out[2 * gi + 1] for gi in range(N_GROUPS)]
    kv_keep_p = kv_out[2 * N_GROUPS:]

    skv_chunks = []
    for gi in range(N_GROUPS):
        skv_chunks.append((True, 1.0, 1, False, (gi, 0)))
        skv_chunks.append((False, 1.0, 1, False, (gi, D_MODEL)))
    kvn, _ = _proj(h_s[None], row(g_kv), w_kv_b, tab_s, tuple(skv_chunks), ((n_s, 2 * D_MODEL),) * N_GROUPS, tm=n_s)
    kvn = [a[0] for a in kvn]

    q_scale = 1.0 / math.sqrt(HEAD_DIM)
    q_chunks = tuple((True, q_scale, dils[gi], True, None) for gi in range(N_GROUPS))
    sq_chunks = tuple((True, q_scale, 1, False, (0, gi * D_MODEL)) for gi in range(N_GROUPS))
    for lj in range(n_attn):
        final = lj == n_attn - 1
        qs = hosted(_proj, h_p, row(g_b[lj]), w_b_in_b[lj], tab_p, q_chunks, (), tm=tm)
        (comb,) = hosted(_attention, qs, ks, vs)
        (h_p,) = hosted(_outproj, h_p, comb, row(g_b[lj]), w_b_in_b[lj], w_b_out_b[lj], row(g_final), final=final, tm=tm)
    assert next(plan, None) is None

    for lj in range(n_attn):
        final = lj == n_attn - 1
        w_gate = w_b_in_b[lj][:, q_cols:]
        (q_s,), _ = _proj(h_s[None], row(g_b[lj]), w_b_in_b[lj], tab_s, sq_chunks, ((n_s, q_cols),), tm=n_s)
        q_s = q_s[0]
        parts = [_sample_attention(q_s[:, gi * D_MODEL:(gi + 1) * D_MODEL], gathered[gi], seg, segt, bb=min(n_s, 16))
                 for gi in range(N_GROUPS)]
        acc = jnp.stack([pt[0] for pt in parts])
        m = jnp.stack([pt[1] for pt in parts])
        l = jnp.stack([pt[2] for pt in parts])
        h_s = _sample_outproj(h_s, q_s, kvn, acc, m, l, seg, segt, row(g_b[lj]), w_gate, w_b_out_b[lj],
                              row(g_final), final=final)

    kv_shape = (2, N_HEADS, HEAD_DIM)
    outs = [h_p, h_s.reshape(n_s, t_s, d), jnp.stack(pool_p), jnp.stack(pool_s)]
    for gi in range(N_GROUPS):
        outs.append(kv_keep_p[gi].reshape(b, keeps[gi], *kv_shape))
        outs.append(kvn[gi].reshape(n_s, t_s, *kv_shape))
    return tuple(outs)
```
